```python
import math
import jax, jax.numpy as jnp
from jax import lax
import numpy as np

D_MODEL = 1024
BATCH = 1
SEQ = 16384
DEPTH = 4

CONV_CH = 256
CONV_GROUPS = 4
CONV_WIDTH = 3
GLA_HEADS = 4
GLA_DK = 32
GLA_DV = 64
GLA_GATE_RANK = 16
GLA_TAU = 16.0
GLA_CHUNK = 64
MLA_HEADS = 4
MLA_Q_RANK = 256
MLA_KV_RANK = 128
MLA_NOPE = 128
MLA_ROPE = 64
MLA_V = 128
ROPE_THETA = 10000.0
ATTN_BLOCK = 128
GLA_QK = GLA_HEADS * GLA_DK
GLA_VW = GLA_HEADS * GLA_DV
MLA_QH = MLA_HEADS * (MLA_NOPE + MLA_ROPE)
MLA_KVH = MLA_HEADS * (MLA_NOPE + MLA_V)
MLA_VW = MLA_HEADS * MLA_V
D_MIX = CONV_CH + GLA_VW + MLA_VW
D_IN = 3 * CONV_CH + 2 * GLA_QK + 2 * GLA_VW + GLA_GATE_RANK + MLA_Q_RANK + MLA_KV_RANK + MLA_ROPE
D_FF = -(-8 * D_MODEL // (3 * 256)) * 256
EPS = 1e-6

kernel_name = "hybrid_conv_gla_mla_parallel_heads"


def rms_norm(x, g):
    xf = x.astype(jnp.float32)
    y = xf * lax.rsqrt(jnp.mean(xf * xf, axis=-1, keepdims=True) + EPS)
    return (y * g.astype(jnp.float32)).astype(x.dtype)


def group_rms_norm(x, g, n_groups):
    shp = x.shape
    xf = x.astype(jnp.float32).reshape(*shp[:-1], n_groups, shp[-1] // n_groups)
    y = xf * lax.rsqrt(jnp.mean(xf * xf, axis=-1, keepdims=True) + EPS)
    return (y.reshape(shp) * g.astype(jnp.float32)).astype(x.dtype)


def rope_tables(positions):
    inv = 1.0 / (ROPE_THETA ** (jnp.arange(0, MLA_ROPE, 2, dtype=jnp.float32) / MLA_ROPE))
    ang = positions.astype(jnp.float32)[..., None] * inv
    return jnp.cos(ang), jnp.sin(ang)


def apply_rope(x, cos, sin):
    cos = cos.astype(x.dtype)
    sin = sin.astype(x.dtype)
    x1, x2 = jnp.split(x, 2, axis=-1)
    return jnp.concatenate([x1 * cos - x2 * sin, x2 * cos + x1 * sin], axis=-1)


def short_conv_mixer(b_gate, c_gate, u, conv_w):
    v = c_gate * u
    w = conv_w[:, None, :].astype(v.dtype)
    y = lax.conv_general_dilated(v, w, window_strides=(1,), padding=[(CONV_WIDTH - 1, 0)],
                                 dimension_numbers=('NWC', 'WIO', 'NWC'),
                                 feature_group_count=CONV_CH)
    return b_gate * y


def gla_mixer(q, k, v, g_low, r, w_gate_up, b_gate, norm_g):
    B, T, _ = q.shape
    H, dk, dv, C = GLA_HEADS, GLA_DK, GLA_DV, GLA_CHUNK
    dtype = v.dtype
    qf = q.astype(jnp.float32).reshape(B, T, H, dk) * (dk ** -0.5)
    kf = k.astype(jnp.float32).reshape(B, T, H, dk)
    vf = v.astype(jnp.float32).reshape(B, T, H, dv)
    log_a = jax.nn.log_sigmoid((g_low @ w_gate_up + b_gate).astype(jnp.float32)) / GLA_TAU
    log_a = log_a.reshape(B, T, H, dk)
    n = T // C

    def to_chunks(t):
        return jnp.moveaxis(t.reshape(B, n, C, H, t.shape[-1]), 1, 0)

    causal = jnp.tril(jnp.ones((C, C), dtype=bool))

    def step(S, inp):
        qi, ki, vi, gi = inp
        bcum = jnp.cumsum(gi, axis=1)
        o_inter = jnp.einsum('bthk,bhkv->bthv', qi * jnp.exp(bcum), S)
        diff = bcum[:, :, None] - bcum[:, None, :]
        decay = jnp.exp(jnp.where(causal[None, :, :, None, None], diff, -jnp.inf))
        A = jnp.einsum('bthk,bshk,btshk->bhts', qi, ki, decay)
        o_intra = jnp.einsum('bhts,bshv->bthv', A, vi)
        b_last = bcum[:, -1]
        k_dec = ki * jnp.exp(b_last[:, None] - bcum)
        S_new = jnp.exp(b_last)[..., None] * S + jnp.einsum('bshk,bshv->bhkv', k_dec, vi)
        return S_new, o_inter + o_intra

    S0 = jnp.zeros((B, H, dk, dv), jnp.float32)
    _, o = lax.scan(step, S0, (to_chunks(qf), to_chunks(kf), to_chunks(vf), to_chunks(log_a)))
    o = jnp.moveaxis(o, 0, 1).reshape(B, T, H * dv).astype(dtype)
    return group_rms_norm(o, norm_g, H) * jax.nn.silu(r)


def mla_mixer(c_q, c_kv, k_rope_in, q_norm_g, w_uq, kv_norm_g, w_ukv, cos, sin):
    B, T, _ = c_q.shape
    H = MLA_HEADS
    q = (rms_norm(c_q, q_norm_g) @ w_uq).reshape(B, T, H, MLA_NOPE + MLA_ROPE)
    q_nope, q_rope = q[..., :MLA_NOPE], q[..., MLA_NOPE:]
    q_rope = apply_rope(q_rope, cos[:, :, None], sin[:, :, None])
    kv = (rms_norm(c_kv, kv_norm_g) @ w_ukv).reshape(B, T, H, MLA_NOPE + MLA_V)
    k_nope, v = kv[..., :MLA_NOPE], kv[..., MLA_NOPE:]
    k_rope = apply_rope(k_rope_in, cos, sin)
    scale = (MLA_NOPE + MLA_ROPE) ** -0.5
    nb = T // ATTN_BLOCK
    qn_b = jnp.moveaxis(q_nope.reshape(B, nb, ATTN_BLOCK, H, MLA_NOPE), 1, 0)
    qr_b = jnp.moveaxis(q_rope.reshape(B, nb, ATTN_BLOCK, H, MLA_ROPE), 1, 0)
    kpos = jnp.arange(T)

    def attend(inp):
        qn, qr, i = inp
        s = (jnp.einsum('bqhd,bkhd->bhqk', qn, k_nope, preferred_element_type=jnp.float32)
             + jnp.einsum('bqhr,bkr->bhqk', qr, k_rope, preferred_element_type=jnp.float32)) * scale
        qpos = i * ATTN_BLOCK + jnp.arange(ATTN_BLOCK)
        s = jnp.where(kpos[None, :] <= qpos[:, None], s, -jnp.inf)
        p = jax.nn.softmax(s, axis=-1).astype(v.dtype)
        return jnp.einsum('bhqk,bkhv->bqhv', p, v)

    o = lax.map(attend, (qn_b, qr_b, jnp.arange(nb)))
    return jnp.moveaxis(o, 0, 1).reshape(B, T, H * MLA_V)


def split_columns(z):
    sizes = (CONV_CH, CONV_CH, CONV_CH, GLA_QK, GLA_QK, GLA_VW, GLA_GATE_RANK, GLA_VW,
             MLA_Q_RANK, MLA_KV_RANK, MLA_ROPE)
    offsets = tuple(int(o) for o in np.cumsum(sizes)[:-1])
    return jnp.split(z, offsets, axis=-1)


def hybrid_layer(x, cos, sin, attn_norm_g, w_in, conv_w, conv_norm_g, gla_w_gate, gla_b_gate,
                 gla_norm_g, mla_q_norm_g, mla_w_uq, mla_kv_norm_g, mla_w_ukv, mla_out_norm_g,
                 w_out, ffn_norm_g, w_gate, w_up, w_down):
    h = rms_norm(x, attn_norm_g)
    z = h @ w_in
    (cb, cc, cu, gq, gk, gv, g_low, gr, c_q, c_kv, k_rope) = split_columns(z)
    y_conv = group_rms_norm(short_conv_mixer(cb, cc, cu, conv_w), conv_norm_g, CONV_GROUPS)
    y_gla = gla_mixer(gq, gk, gv, g_low, gr, gla_w_gate, gla_b_gate, gla_norm_g)
    y_mla = group_rms_norm(mla_mixer(c_q, c_kv, k_rope, mla_q_norm_g, mla_w_uq, mla_kv_norm_g,
                                     mla_w_ukv, cos, sin), mla_out_norm_g, MLA_HEADS)
    x = x + jnp.concatenate([y_conv, y_gla, y_mla], axis=-1) @ w_out
    h = rms_norm(x, ffn_norm_g)
    return x + (jax.nn.silu(h @ w_gate) * (h @ w_up)) @ w_down


def setup_inputs(seed: int = 0) -> dict:
    key = jax.random.key(seed)
    ks = jax.random.split(key, 24)
    f32 = jnp.float32
    L = DEPTH

    def w(k, shape, fan_in, mult=1.0):
        return jax.random.normal(k, shape, f32) * (fan_in ** -0.5) * mult

    def gain(k, shape):
        return 1.0 + 0.01 * jax.random.normal(k, shape, f32)

    res_mult = (2.0 * DEPTH) ** -0.5
    return {
        "x": jax.random.normal(ks[0], (BATCH, SEQ, D_MODEL), f32),
        "positions": jnp.broadcast_to(jnp.arange(SEQ, dtype=jnp.int32), (BATCH, SEQ)),
        "attn_norm_g": gain(ks[1], (L, D_MODEL)),
        "w_in": w(ks[2], (L, D_MODEL, D_IN), D_MODEL),
        "conv_w": w(ks[3], (L, CONV_WIDTH, CONV_CH), CONV_WIDTH),
        "conv_norm_g": gain(ks[4], (L, CONV_CH)),
        "gla_w_gate": w(ks[5], (L, GLA_GATE_RANK, GLA_QK), GLA_GATE_RANK),
        "gla_b_gate": 0.01 * jax.random.normal(ks[6], (L, GLA_QK), f32),
        "gla_norm_g": gain(ks[7], (L, GLA_VW)),
        "mla_q_norm_g": gain(ks[8], (L, MLA_Q_RANK)),
        "mla_w_uq": w(ks[9], (L, MLA_Q_RANK, MLA_QH), MLA_Q_RANK),
        "mla_kv_norm_g": gain(ks[10], (L, MLA_KV_RANK)),
        "mla_w_ukv": w(ks[11], (L, MLA_KV_RANK, MLA_KVH), MLA_KV_RANK),
        "mla_out_norm_g": gain(ks[12], (L, MLA_VW)),
        "w_out": w(ks[13], (L, D_MIX, D_MODEL), D_MIX, res_mult),
        "ffn_norm_g": gain(ks[14], (L, D_MODEL)),
        "w_gate": w(ks[15], (L, D_MODEL, D_FF), D_MODEL),
        "w_up": w(ks[16], (L, D_MODEL, D_FF), D_MODEL),
        "w_down": w(ks[17], (L, D_FF, D_MODEL), D_FF, res_mult),
        "final_norm_g": gain(ks[18], (D_MODEL,)),
    }


def reference(x, positions, attn_norm_g, w_in, conv_w, conv_norm_g, gla_w_gate, gla_b_gate,
              gla_norm_g, mla_q_norm_g, mla_w_uq, mla_kv_norm_g, mla_w_ukv, mla_out_norm_g,
              w_out, ffn_norm_g, w_gate, w_up, w_down, final_norm_g):
    cos, sin = rope_tables(positions)
    for l in range(DEPTH):
        x = hybrid_layer(x, cos, sin, attn_norm_g[l], w_in[l], conv_w[l], conv_norm_g[l],
                         gla_w_gate[l], gla_b_gate[l], gla_norm_g[l], mla_q_norm_g[l],
                         mla_w_uq[l], mla_kv_norm_g[l], mla_w_ukv[l], mla_out_norm_g[l],
                         w_out[l], ffn_norm_g[l], w_gate[l], w_up[l], w_down[l])
    return rms_norm(x, final_norm_g)
```

```python
import functools

import jax
import jax.numpy as jnp
from jax import lax
from jax.experimental import pallas as pl
from jax.experimental.pallas import tpu as pltpu

D_MODEL = 1024
CONV_CH = 256
CONV_GROUP_W = 64
GLA_HEADS = 4
GLA_DK = 32
GLA_DV = 64
GLA_GATE_RANK = 16
GLA_TAU = 16.0
GLA_QK = GLA_HEADS * GLA_DK
GLA_VW = GLA_HEADS * GLA_DV
GLA_BLOCK = 16
MLA_HEADS = 4
MLA_Q_RANK = 256
MLA_KV_RANK = 128
MLA_NOPE = 128
MLA_ROPE = 64
MLA_V = 128
MLA_QK_PAD = 256
ROPE_THETA = 10000.0
D_FF = 2816
EPS = 1e-6

C_CONV = 0
C_QK = 768
C_GV = 1024
C_GR = 1280
C_GLOW = 1536
C_CQ = 1664
C_CKV = 1920
C_KR = 2048
D_IN_PACKED = 2176

V7X_VMEM_LIMIT_BYTES = 58 * 1024 * 1024

F32 = jnp.float32
BF16 = jnp.bfloat16
NEG_BIG = -1e30


def _dot(a, b):
    return jnp.dot(a, b, preferred_element_type=F32)


def _dot_nt(a, b):
    return lax.dot_general(a, b, (((1,), (1,)), ((), ())), preferred_element_type=F32)


def _dot_tn(a, b):
    return lax.dot_general(a, b, (((0,), (0,)), ((), ())), preferred_element_type=F32)


def _split3(x):
    hi = x.astype(BF16)
    r = x - hi.astype(F32)
    mid = r.astype(BF16)
    lo = (r - mid.astype(F32)).astype(BF16)
    return hi, mid, lo


def _rms(x, g):
    return x * lax.rsqrt(jnp.mean(x * x, axis=-1, keepdims=True) + EPS) * g


def _group_indicator(n, width):
    r = lax.broadcasted_iota(jnp.int32, (n, n), 0) // width
    c = lax.broadcasted_iota(jnp.int32, (n, n), 1) // width
    return (r == c).astype(BF16)


def _group_rms(x, ind, width):
    sq = x * x
    hi = sq.astype(BF16)
    lo = (sq - hi.astype(F32)).astype(BF16)
    ms = (_dot(hi, ind) + _dot(lo, ind)) * (1.0 / width)
    return x * lax.rsqrt(ms + EPS)


def _sigmoid(x):
    return 1.0 / (1.0 + jnp.exp(-x))


def _rope_kernel(pos_ref, inv_ref, tab_ref):
    ang = pos_ref[...].astype(F32) * inv_ref[...]
    lane = lax.broadcasted_iota(jnp.int32, ang.shape, 1)
    c = jnp.cos(ang)
    s = jnp.sin(ang)
    tab_ref[...] = jnp.where(lane < 64, c, jnp.where(lane < 96, -s, s))


def _rope_table(positions, t):
    tr = min(t, 1024)
    inv = 1.0 / (ROPE_THETA ** (jnp.arange(0, MLA_ROPE, 2, dtype=F32) / MLA_ROPE))
    inv4 = jnp.tile(inv, 4).reshape(1, 128)
    return pl.pallas_call(
        _rope_kernel,
        grid=(t // tr,),
        in_specs=[pl.BlockSpec((tr, 1), lambda i: (i, 0)),
                  pl.BlockSpec((1, 128), lambda i: (0, 0))],
        out_specs=pl.BlockSpec((tr, 128), lambda i: (i, 0)),
        out_shape=jax.ShapeDtypeStruct((t, 128), F32),
        name="rope_table",
    )(positions.reshape(t, 1), inv4)


def _mixer_kernel(x_ref, tab_ref, ang_ref, win_ref, convw_ref, convg_ref, wg_ref, bg_ref, glag_ref,
                  qg_ref, wuq_ref, kvg_ref, wukv_ref,
                  ycg_ref, q_ref, k_ref, v_ref,
                  vbuf, st_ref, qs_ref, ks_ref, vs_ref, dec_ref, oi_ref, *, tm):
    nb = tm // GLA_BLOCK

    @pl.when(pl.program_id(0) == 0)
    def _():
        vbuf[0:8, :] = jnp.zeros((8, CONV_CH), F32)
        st_ref[...] = jnp.zeros(st_ref.shape, F32)

    h = _rms(x_ref[...], ang_ref[...]).astype(BF16)

    def proj(c0, n):
        return _dot(h, win_ref[:, c0:c0 + n])

    ind64 = _group_indicator(256, 64)

    zc = proj(C_CONV, 3 * CONV_CH)
    cb = zc[:, 0:CONV_CH]
    v = zc[:, CONV_CH:2 * CONV_CH] * zc[:, 2 * CONV_CH:3 * CONV_CH]
    vbuf[8:8 + tm, :] = v
    y = (convw_ref[2:3, :] * v + convw_ref[1:2, :] * vbuf[7:7 + tm, :]
         + convw_ref[0:1, :] * vbuf[6:6 + tm, :])
    vbuf[0:8, :] = vbuf[tm:tm + 8, :]
    y_conv = _group_rms(cb * y, ind64, CONV_GROUP_W) * convg_ref[...]
    ycg_ref[:, 0:CONV_CH] = y_conv.astype(BF16)

    zqk = proj(C_QK, 2 * GLA_QK)
    gq = zqk[:, 0:GLA_QK] * (GLA_DK ** -0.5)
    gk = zqk[:, GLA_QK:2 * GLA_QK]
    gv = proj(C_GV, GLA_VW)
    glow = proj(C_GLOW, 128)
    xg = _dot(glow.astype(BF16), wg_ref[...]) + bg_ref[...]
    la = (jnp.minimum(xg, 0.0) - jnp.log(1.0 + jnp.exp(-jnp.abs(xg)))) * (1.0 / GLA_TAU)

    rr = lax.broadcasted_iota(jnp.int32, (128, 128), 0)
    cc = lax.broadcasted_iota(jnp.int32, (128, 128), 1)
    same = (rr // GLA_BLOCK) == (cc // GLA_BLOCK)
    tri = (same & (cc <= rr)).astype(BF16)
    blk = same.astype(BF16)
    parts = _split3(la)
    b_rows, e_rows = [], []
    for g in range(tm // 128):
        sl = slice(g * 128, (g + 1) * 128)
        b_rows.append(_dot(tri, parts[0][sl]) + _dot(tri, parts[1][sl]) + _dot(tri, parts[2][sl]))
        e_rows.append(_dot(blk, parts[0][sl]) + _dot(blk, parts[1][sl]) + _dot(blk, parts[2][sl]))
    b = jnp.concatenate(b_rows, axis=0)
    bend = jnp.concatenate(e_rows, axis=0)

    hr = lax.broadcasted_iota(jnp.int32, (GLA_QK, GLA_VW), 0) // GLA_DK
    hc = lax.broadcasted_iota(jnp.int32, (GLA_QK, GLA_VW), 1) // GLA_DV
    expand = (hr == hc).astype(BF16)
    tpos = lax.broadcasted_iota(jnp.int32, (tm, GLA_QK), 0) % GLA_BLOCK
    od = _dot((gq * gk).astype(BF16), expand) * gv
    for d in range(1, GLA_BLOCK):
        arg = jnp.where(tpos >= d, b - pltpu.roll(b, d, axis=0), NEG_BIG)
        p = gq * jnp.exp(arg) * pltpu.roll(gk, d, axis=0)
        od = od + _dot(p.astype(BF16), expand) * pltpu.roll(gv, d, axis=0)

    qs_ref[...] = (gq * jnp.exp(b)).astype(BF16)
    ks_ref[...] = (gk * jnp.exp(bend - b)).astype(BF16)
    vs_ref[...] = gv.astype(BF16)
    dec_ref[...] = jnp.exp(bend)
    mr = lax.broadcasted_iota(jnp.int32, (GLA_VW, GLA_QK), 0) // GLA_DV
    mc = lax.broadcasted_iota(jnp.int32, (GLA_VW, GLA_QK), 1) // GLA_DK
    head_mask = (mr == mc).astype(F32)

    def block_step(j, st):
        r0 = pl.multiple_of(j * GLA_BLOCK, GLA_BLOCK)
        qj = qs_ref[pl.ds(r0, GLA_BLOCK), :]
        kj = ks_ref[pl.ds(r0, GLA_BLOCK), :]
        vj = vs_ref[pl.ds(r0, GLA_BLOCK), :]
        dj = dec_ref[pl.ds(r0, 1), :]
        oi_ref[pl.ds(r0, GLA_BLOCK), :] = _dot_nt(qj, st.astype(BF16))
        return st * dj + _dot_tn(vj, kj) * head_mask

    st_ref[...] = lax.fori_loop(0, nb, block_step, st_ref[...])

    gr = proj(C_GR, GLA_VW)
    o = oi_ref[...] + od
    y_gla = _group_rms(o, ind64, GLA_DV) * glag_ref[...] * (gr * _sigmoid(gr))
    ycg_ref[:, CONV_CH:CONV_CH + GLA_VW] = y_gla.astype(BF16)

    tab = tab_ref[...]
    qa = _dot(_rms(proj(C_CQ, MLA_Q_RANK), qg_ref[...]).astype(BF16), wuq_ref[...])
    kva = _dot(_rms(proj(C_CKV, MLA_KV_RANK), kvg_ref[...]).astype(BF16), wukv_ref[...])
    kr = proj(C_KR, 128) * tab
    lane = lax.broadcasted_iota(jnp.int32, (tm, 128), 1)
    kro = jnp.where(lane < MLA_ROPE, kr + pltpu.roll(kr, 64, axis=1), 0.0).astype(BF16)
    for hd in range(MLA_HEADS):
        c0 = hd * MLA_QK_PAD
        q_ref[:, c0:c0 + 128] = qa[:, c0:c0 + 128].astype(BF16)
        qr = qa[:, c0 + 128:c0 + 256] * tab
        q_ref[:, c0 + 128:c0 + 256] = (qr + pltpu.roll(qr, 64, axis=1)).astype(BF16)
        k_ref[:, c0:c0 + 128] = kva[:, c0:c0 + 128].astype(BF16)
        k_ref[:, c0 + 128:c0 + 256] = kro
        v_ref[:, hd * MLA_V:(hd + 1) * MLA_V] = kva[:, c0 + 128:c0 + 256].astype(BF16)


def _mixer_in(x, tab, lw, t, tm):
    row = lambda w: pl.BlockSpec((tm, w), lambda i: (i, 0))
    full = lambda a: pl.BlockSpec(a.shape, lambda i: (0,) * a.ndim)
    consts = [lw["attn_norm_g"], lw["w_in"], lw["conv_w"], lw["conv_norm_g"], lw["gla_w_gate"],
              lw["gla_b_gate"], lw["gla_norm_g"], lw["mla_q_norm_g"], lw["mla_w_uq"],
              lw["mla_kv_norm_g"], lw["mla_w_ukv"]]
    return pl.pallas_call(
        functools.partial(_mixer_kernel, tm=tm),
        grid=(t // tm,),
        in_specs=[row(D_MODEL), row(128)] + [full(a) for a in consts],
        out_specs=[row(CONV_CH + GLA_VW), row(MLA_HEADS * MLA_QK_PAD), row(MLA_HEADS * MLA_QK_PAD),
                   row(MLA_HEADS * MLA_V)],
        out_shape=[jax.ShapeDtypeStruct((t, CONV_CH + GLA_VW), BF16),
                   jax.ShapeDtypeStruct((t, MLA_HEADS * MLA_QK_PAD), BF16),
                   jax.ShapeDtypeStruct((t, MLA_HEADS * MLA_QK_PAD), BF16),
                   jax.ShapeDtypeStruct((t, MLA_HEADS * MLA_V), BF16)],
        scratch_shapes=[pltpu.VMEM((tm + 8, CONV_CH), F32),
                        pltpu.VMEM((GLA_VW, GLA_QK), F32),
                        pltpu.VMEM((tm, GLA_QK), BF16),
                        pltpu.VMEM((tm, GLA_QK), BF16),
                        pltpu.VMEM((tm, GLA_VW), BF16),
                        pltpu.VMEM((tm, GLA_QK), F32),
                        pltpu.VMEM((tm, GLA_VW), F32)],
        compiler_params=pltpu.CompilerParams(dimension_semantics=("arbitrary",),
                                             vmem_limit_bytes=V7X_VMEM_LIMIT_BYTES),
        name="mixer_in",
    )(x, tab, *consts)


def _attn_kernel(q_ref, k_ref, v_ref, g_ref, o_ref, m_sc, l_sc, acc_sc, *, blk, log2_scale):
    i = pl.program_id(1)
    q = q_ref[...]
    m_sc[...] = jnp.full(m_sc.shape, -jnp.inf, F32)
    l_sc[...] = jnp.zeros(l_sc.shape, F32)
    acc_sc[...] = jnp.zeros(acc_sc.shape, F32)

    def step(j, diagonal):
        r0 = pl.multiple_of(j * blk, blk)
        s = _dot_nt(q, k_ref[pl.ds(r0, blk), :])
        if diagonal:
            rows = lax.broadcasted_iota(jnp.int32, (blk, blk), 0)
            cols = lax.broadcasted_iota(jnp.int32, (blk, blk), 1)
            s = jnp.where(cols <= rows, s, -jnp.inf)
        m_prev = m_sc[...]
        m_new = jnp.maximum(m_prev, jnp.max(s, axis=-1, keepdims=True))
        p = jnp.exp2((s - m_new) * log2_scale)
        alpha = jnp.exp2((m_prev - m_new) * log2_scale)
        l_sc[...] = alpha * l_sc[...] + jnp.sum(p, axis=-1, keepdims=True)
        acc_sc[...] = alpha * acc_sc[...] + _dot(p.astype(BF16), v_ref[pl.ds(r0, blk), :])
        m_sc[...] = m_new

    def off_diagonal(j, carry):
        step(j, False)
        return carry

    lax.fori_loop(0, i, off_diagonal, 0)
    step(i, True)
    o = acc_sc[...] / l_sc[...]
    o_ref[...] = _rms(o, g_ref[...]).astype(BF16)


def _mla_attn(q, k, v, out_norm_g, t, blk):
    scale = (MLA_NOPE + MLA_ROPE) ** -0.5
    return pl.pallas_call(
        functools.partial(_attn_kernel, blk=blk, log2_scale=scale * 1.4426950408889634),
        grid=(MLA_HEADS, t // blk),
        in_specs=[pl.BlockSpec((blk, MLA_QK_PAD), lambda h, i: (i, h)),
                  pl.BlockSpec((t, MLA_QK_PAD), lambda h, i: (0, h)),
                  pl.BlockSpec((t, MLA_V), lambda h, i: (0, h)),
                  pl.BlockSpec((1, MLA_V), lambda h, i: (0, h))],
        out_specs=pl.BlockSpec((blk, MLA_V), lambda h, i: (i, h)),
        out_shape=jax.ShapeDtypeStruct((t, MLA_HEADS * MLA_V), BF16),
        scratch_shapes=[pltpu.VMEM((blk, 1), F32), pltpu.VMEM((blk, 1), F32),
                        pltpu.VMEM((blk, MLA_V), F32)],
        compiler_params=pltpu.CompilerParams(dimension_semantics=("arbitrary", "arbitrary"),
                                             vmem_limit_bytes=V7X_VMEM_LIMIT_BYTES),
        name="mla_attn",
    )(q, k, v, out_norm_g)


def _out_ffn_kernel(x_ref, ycg_ref, ymla_ref, wout_ref, fg_ref, wgate_ref, wup_ref, wdown_ref,
                    og_ref, o_ref, *, final_norm):
    half = CONV_CH + GLA_VW
    x1 = (x_ref[...] + _dot(ycg_ref[...], wout_ref[0:half, :])
          + _dot(ymla_ref[...], wout_ref[half:2 * half, :]))
    h = _rms(x1, fg_ref[...]).astype(BF16)
    g = _dot(h, wgate_ref[...])
    u = _dot(h, wup_ref[...])
    a = (g * _sigmoid(g) * u).astype(BF16)
    out = x1 + _dot(a, wdown_ref[...])
    if final_norm:
        out = _rms(out, og_ref[...])
    o_ref[...] = out


def _out_ffn(x, ycg, ymla, lw, final_g, t, tm, final_norm):
    row = lambda w: pl.BlockSpec((tm, w), lambda i: (i, 0))
    resident = pl.BlockSpec(memory_space=pltpu.VMEM)
    return pl.pallas_call(
        functools.partial(_out_ffn_kernel, final_norm=final_norm),
        grid=(t // tm,),
        in_specs=[row(D_MODEL), row(CONV_CH + GLA_VW), row(MLA_HEADS * MLA_V),
                  resident, resident, resident, resident, resident, resident],
        out_specs=row(D_MODEL),
        out_shape=jax.ShapeDtypeStruct((t, D_MODEL), F32),
        compiler_params=pltpu.CompilerParams(dimension_semantics=("arbitrary",),
                                             vmem_limit_bytes=V7X_VMEM_LIMIT_BYTES),
        name="out_ffn",
    )(x, ycg, ymla, lw["w_out"], lw["ffn_norm_g"], lw["w_gate"], lw["w_up"], lw["w_down"], final_g)


def _pack_w_in(w):
    glow = jnp.pad(w[..., 1280:1296], ((0, 0), (0, 0), (0, 128 - GLA_GATE_RANK)))
    return jnp.concatenate(
        [w[..., 0:768], w[..., 768:1024], w[..., 1024:1280], w[..., 1296:1552], glow,
         w[..., 1552:1808], w[..., 1808:1936], w[..., 1936:2000], w[..., 1968:2000],
         w[..., 1936:1968]], axis=-1).astype(BF16)


def _pack_w_uq(w):
    depth = w.shape[0]
    w = w.reshape(depth, MLA_Q_RANK, MLA_HEADS, MLA_NOPE + MLA_ROPE)
    nope, r1, r2 = w[..., :MLA_NOPE], w[..., MLA_NOPE:MLA_NOPE + 32], w[..., MLA_NOPE + 32:]
    return jnp.concatenate([nope, r1, r2, r2, r1], axis=-1).reshape(
        depth, MLA_Q_RANK, MLA_HEADS * MLA_QK_PAD).astype(BF16)


def kernel(x, positions, attn_norm_g, w_in, conv_w, conv_norm_g, gla_w_gate, gla_b_gate, gla_norm_g, mla_q_norm_g, mla_w_uq, mla_kv_norm_g, mla_w_ukv, mla_out_norm_g, w_out, ffn_norm_g, w_gate, w_up, w_down, final_norm_g):
    batch, t, _ = x.shape
    assert batch == 1
    depth = w_in.shape[0]
    tm = min(t, 512)
    blk = min(t, 512)
    assert t % tm == 0 and t % blk == 0 and tm % 128 == 0

    row = lambda a: a.reshape(depth, 1, a.shape[-1])
    packed = {
        "attn_norm_g": row(attn_norm_g),
        "w_in": _pack_w_in(w_in),
        "conv_w": jnp.pad(conv_w, ((0, 0), (0, 8 - conv_w.shape[1]), (0, 0))),
        "conv_norm_g": row(conv_norm_g),
        "gla_w_gate": jnp.pad(gla_w_gate, ((0, 0), (0, 128 - GLA_GATE_RANK), (0, 0))).astype(BF16),
        "gla_b_gate": row(gla_b_gate),
        "gla_norm_g": row(gla_norm_g),
        "mla_q_norm_g": row(mla_q_norm_g),
        "mla_w_uq": _pack_w_uq(mla_w_uq),
        "mla_kv_norm_g": row(mla_kv_norm_g),
        "mla_w_ukv": mla_w_ukv.astype(BF16),
        "mla_out_norm_g": row(mla_out_norm_g),
        "w_out": w_out.astype(BF16),
        "ffn_norm_g": row(ffn_norm_g),
        "w_gate": w_gate.astype(BF16),
        "w_up": w_up.astype(BF16),
        "w_down": w_down.astype(BF16),
    }
    final_g = final_norm_g.reshape(1, D_MODEL)

    tab = _rope_table(positions, t)
    xs = x.reshape(t, D_MODEL)
    for l in range(depth):
        lw = {name: a[l] for name, a in packed.items()}
        ycg, q, k, v = _mixer_in(xs, tab, lw, t, tm)
        ymla = _mla_attn(q, k, v, lw["mla_out_norm_g"], t, blk)
        xs = _out_ffn(xs, ycg, ymla, lw, final_g, t, tm, final_norm=(l == depth - 1))
    return xs.reshape(batch, t, D_MODEL)
```

```python
import functools

import jax
import jax.numpy as jnp
from jax import lax
from jax.experimental import pallas as pl
from jax.experimental.pallas import tpu as pltpu

D_MODEL = 1024
CONV_CH = 256
CONV_GROUP_W = 64
GLA_HEADS = 4
GLA_DK = 32
GLA_DV = 64
GLA_GATE_RANK = 16
GLA_TAU = 16.0
GLA_QK = GLA_HEADS * GLA_DK
GLA_VW = GLA_HEADS * GLA_DV
GLA_BLOCK = 16
MLA_HEADS = 4
MLA_Q_RANK = 256
MLA_KV_RANK = 128
MLA_NOPE = 128
MLA_ROPE = 64
MLA_V = 128
MLA_QK_PAD = 256
ROPE_THETA = 10000.0
D_FF = 2816
EPS = 1e-6

C_CONV = 0
C_QK = 768
C_GV = 1024
C_GR = 1280
C_GLOW = 1536
C_CQ = 1664
C_CKV = 1920
C_KR = 2048
D_IN_PACKED = 2176

V7X_VMEM_LIMIT_BYTES = 58 * 1024 * 1024

F32 = jnp.float32
BF16 = jnp.bfloat16
NEG_BIG = -1e30


def _dot(a, b):
    return jnp.dot(a, b, preferred_element_type=F32)


def _dot_nt(a, b):
    return lax.dot_general(a, b, (((1,), (1,)), ((), ())), preferred_element_type=F32)


def _dot_tn(a, b):
    return lax.dot_general(a, b, (((0,), (0,)), ((), ())), preferred_element_type=F32)


def _split3(x):
    hi = x.astype(BF16)
    r = x - hi.astype(F32)
    mid = r.astype(BF16)
    lo = (r - mid.astype(F32)).astype(BF16)
    return hi, mid, lo


def _rms(x, g):
    return x * lax.rsqrt(jnp.mean(x * x, axis=-1, keepdims=True) + EPS) * g


def _group_indicator(n, width):
    r = lax.broadcasted_iota(jnp.int32, (n, n), 0) // width
    c = lax.broadcasted_iota(jnp.int32, (n, n), 1) // width
    return (r == c).astype(BF16)


def _group_rms(x, ind, width):
    sq = x * x
    hi = sq.astype(BF16)
    lo = (sq - hi.astype(F32)).astype(BF16)
    ms = (_dot(hi, ind) + _dot(lo, ind)) * (1.0 / width)
    return x * lax.rsqrt(ms + EPS)


def _sigmoid(x):
    return 1.0 / (1.0 + jnp.exp(-x))


def _rope_kernel(pos_ref, inv_ref, tab_ref):
    ang = pos_ref[...].astype(F32) * inv_ref[...]
    lane = lax.broadcasted_iota(jnp.int32, ang.shape, 1)
    c = jnp.cos(ang)
    s = jnp.sin(ang)
    tab_ref[...] = jnp.where(lane < 64, c, jnp.where(lane < 96, -s, s))


def _rope_table(positions, t):
    tr = min(t, 1024)
    inv = 1.0 / (ROPE_THETA ** (jnp.arange(0, MLA_ROPE, 2, dtype=F32) / MLA_ROPE))
    inv4 = jnp.tile(inv, 4).reshape(1, 128)
    return pl.pallas_call(
        _rope_kernel,
        grid=(t // tr,),
        in_specs=[pl.BlockSpec((tr, 1), lambda i: (i, 0)),
                  pl.BlockSpec((1, 128), lambda i: (0, 0))],
        out_specs=pl.BlockSpec((tr, 128), lambda i: (i, 0)),
        out_shape=jax.ShapeDtypeStruct((t, 128), F32),
        name="rope_table",
    )(positions.reshape(t, 1), inv4)


def _mixer_kernel(x_ref, tab_ref, ang_ref, win_ref, convw_ref, convg_ref, wg_ref, bg_ref, glag_ref,
                  qg_ref, wuq_ref, kvg_ref, wukv_ref,
                  ycg_ref, q_ref, k_ref, v_ref,
                  vbuf, st_ref, qs_ref, ks_ref, vs_ref, dec_ref, oi_ref, *, tm):
    nb = tm // GLA_BLOCK

    @pl.when(pl.program_id(0) == 0)
    def _():
        vbuf[0:8, :] = jnp.zeros((8, CONV_CH), F32)
        st_ref[...] = jnp.zeros(st_ref.shape, F32)

    h = _rms(x_ref[...], ang_ref[...]).astype(BF16)

    def proj(c0, n):
        return _dot(h, win_ref[:, c0:c0 + n])

    ind64 = _group_indicator(256, 64)

    zc = proj(C_CONV, 3 * CONV_CH)
    cb = zc[:, 0:CONV_CH]
    v = zc[:, CONV_CH:2 * CONV_CH] * zc[:, 2 * CONV_CH:3 * CONV_CH]
    vbuf[8:8 + tm, :] = v
    y = (convw_ref[2:3, :] * v + convw_ref[1:2, :] * vbuf[7:7 + tm, :]
         + convw_ref[0:1, :] * vbuf[6:6 + tm, :])
    vbuf[0:8, :] = vbuf[tm:tm + 8, :]
    y_conv = _group_rms(cb * y, ind64, CONV_GROUP_W) * convg_ref[...]
    ycg_ref[:, 0:CONV_CH] = y_conv.astype(BF16)

    zqk = proj(C_QK, 2 * GLA_QK)
    gq = zqk[:, 0:GLA_QK] * (GLA_DK ** -0.5)
    gk = zqk[:, GLA_QK:2 * GLA_QK]
    gv = proj(C_GV, GLA_VW)
    glow = proj(C_GLOW, 128)
    xg = _dot(glow.astype(BF16), wg_ref[...]) + bg_ref[...]
    la = (jnp.minimum(xg, 0.0) - jnp.log(1.0 + jnp.exp(-jnp.abs(xg)))) * (1.0 / GLA_TAU)

    rr = lax.broadcasted_iota(jnp.int32, (128, 128), 0)
    cc = lax.broadcasted_iota(jnp.int32, (128, 128), 1)
    same = (rr // GLA_BLOCK) == (cc // GLA_BLOCK)
    tri = (same & (cc <= rr)).astype(BF16)
    blk = same.astype(BF16)
    parts = _split3(la)
    b_rows, e_rows = [], []
    for g in range(tm // 128):
        sl = slice(g * 128, (g + 1) * 128)
        b_rows.append(_dot(tri, parts[0][sl]) + _dot(tri, parts[1][sl]) + _dot(tri, parts[2][sl]))
        e_rows.append(_dot(blk, parts[0][sl]) + _dot(blk, parts[1][sl]) + _dot(blk, parts[2][sl]))
    b = jnp.concatenate(b_rows, axis=0)
    bend = jnp.concatenate(e_rows, axis=0)

    hr = lax.broadcasted_iota(jnp.int32, (GLA_QK, GLA_VW), 0) // GLA_DK
    hc = lax.broadcasted_iota(jnp.int32, (GLA_QK, GLA_VW), 1) // GLA_DV
    expand = (hr == hc).astype(BF16)
    tpos = lax.broadcasted_iota(jnp.int32, (tm, GLA_QK), 0) % GLA_BLOCK
    od = _dot((gq * gk).astype(BF16), expand) * gv
    for d in range(1, GLA_BLOCK):
        arg = jnp.where(tpos >= d, b - pltpu.roll(b, d, axis=0), NEG_BIG)
        p = gq * jnp.exp(arg) * pltpu.roll(gk, d, axis=0)
        od = od + _dot(p.astype(BF16), expand) * pltpu.roll(gv, d, axis=0)

    qs_ref[...] = (gq * jnp.exp(b)).astype(BF16)
    ks_ref[...] = (gk * jnp.exp(bend - b)).astype(BF16)
    vs_ref[...] = gv.astype(BF16)
    dec_ref[...] = jnp.exp(bend)
    mr = lax.broadcasted_iota(jnp.int32, (GLA_VW, GLA_QK), 0) // GLA_DV
    mc = lax.broadcasted_iota(jnp.int32, (GLA_VW, GLA_QK), 1) // GLA_DK
    head_mask = (mr == mc).astype(F32)

    def block_step(j, st):
        r0 = pl.multiple_of(j * GLA_BLOCK, GLA_BLOCK)
        qj = qs_ref[pl.ds(r0, GLA_BLOCK), :]
        kj = ks_ref[pl.ds(r0, GLA_BLOCK), :]
        vj = vs_ref[pl.ds(r0, GLA_BLOCK), :]
        dj = dec_ref[pl.ds(r0, 1), :]
        oi_ref[pl.ds(r0, GLA_BLOCK), :] = _dot_nt(qj, st.astype(BF16))
        return st * dj + _dot_tn(vj, kj) * head_mask

    st_ref[...] = lax.fori_loop(0, nb, block_step, st_ref[...])

    gr = proj(C_GR, GLA_VW)
    o = oi_ref[...] + od
    y_gla = _group_rms(o, ind64, GLA_DV) * glag_ref[...] * (gr * _sigmoid(gr))
    ycg_ref[:, CONV_CH:CONV_CH + GLA_VW] = y_gla.astype(BF16)

    tab = tab_ref[...]
    qa = _dot(_rms(proj(C_CQ, MLA_Q_RANK), qg_ref[...]).astype(BF16), wuq_ref[...])
    kva = _dot(_rms(proj(C_CKV, MLA_KV_RANK), kvg_ref[...]).astype(BF16), wukv_ref[...])
    kr = proj(C_KR, 128) * tab
    lane = lax.broadcasted_iota(jnp.int32, (tm, 128), 1)
    kro = jnp.where(lane < MLA_ROPE, kr + pltpu.roll(kr, 64, axis=1), 0.0).astype(BF16)
    for hd in range(MLA_HEADS):
        c0 = hd * MLA_QK_PAD
        q_ref[:, c0:c0 + 128] = qa[:, c0:c0 + 128].astype(BF16)
        qr = qa[:, c0 + 128:c0 + 256] * tab
        q_ref[:, c0 + 128:c0 + 256] = (qr + pltpu.roll(qr, 64, axis=1)).astype(BF16)
        k_ref[:, c0:c0 + 128] = kva[:, c0:c0 + 128].astype(BF16)
        k_ref[:, c0 + 128:c0 + 256] = kro
        v_ref[:, hd * MLA_V:(hd + 1) * MLA_V] = kva[:, c0 + 128:c0 + 256].astype(BF16)


def _mixer_in(x, tab, lw, t, tm):
    row = lambda w: pl.BlockSpec((tm, w), lambda i: (i, 0))
    full = lambda a: pl.BlockSpec(a.shape, lambda i: (0,) * a.ndim)
    consts = [lw["attn_norm_g"], lw["w_in"], lw["conv_w"], lw["conv_norm_g"], lw["gla_w_gate"],
              lw["gla_b_gate"], lw["gla_norm_g"], lw["mla_q_norm_g"], lw["mla_w_uq"],
              lw["mla_kv_norm_g"], lw["mla_w_ukv"]]
    return pl.pallas_call(
        functools.partial(_mixer_kernel, tm=tm),
        grid=(t // tm,),
        in_specs=[row(D_MODEL), row(128)] + [full(a) for a in consts],
        out_specs=[row(CONV_CH + GLA_VW), row(MLA_HEADS * MLA_QK_PAD), row(MLA_HEADS * MLA_QK_PAD),
                   row(MLA_HEADS * MLA_V)],
        out_shape=[jax.ShapeDtypeStruct((t, CONV_CH + GLA_VW), BF16),
                   jax.ShapeDtypeStruct((t, MLA_HEADS * MLA_QK_PAD), BF16),
                   jax.ShapeDtypeStruct((t, MLA_HEADS * MLA_QK_PAD), BF16),
                   jax.ShapeDtypeStruct((t, MLA_HEADS * MLA_V), BF16)],
        scratch_shapes=[pltpu.VMEM((tm + 8, CONV_CH), F32),
                        pltpu.VMEM((GLA_VW, GLA_QK), F32),
                        pltpu.VMEM((tm, GLA_QK), BF16),
                        pltpu.VMEM((tm, GLA_QK), BF16),
                        pltpu.VMEM((tm, GLA_VW), BF16),
                        pltpu.VMEM((tm, GLA_QK), F32),
                        pltpu.VMEM((tm, GLA_VW), F32)],
        compiler_params=pltpu.CompilerParams(dimension_semantics=("arbitrary",),
                                             vmem_limit_bytes=V7X_VMEM_LIMIT_BYTES),
        name="mixer_in",
    )(x, tab, *consts)


def _attn_kernel(q_ref, k_ref, v_ref, g_ref, o_ref, s0, s1, p0, p1, al0, al1, m_sc, l_sc, acc_sc,
                 *, blk, log2_scale):
    i = pl.program_id(1)
    nch = blk // 128
    s_bufs, p_bufs, al_bufs = (s0, s1), (p0, p1), (al0, al1)
    m_sc[...] = jnp.full(m_sc.shape, -jnp.inf, F32)
    l_sc[...] = jnp.zeros(l_sc.shape, F32)
    acc_sc[...] = jnp.zeros(acc_sc.shape, F32)
    p1[...] = jnp.zeros(p1.shape, BF16)
    al1[...] = jnp.ones(al1.shape, F32)

    def scores(j, slot):
        r0 = pl.multiple_of(j * blk, blk)
        s_bufs[slot][...] = _dot_nt(q_ref[...], k_ref[pl.ds(r0, blk), :])

    def softmax(slot, diagonal):
        chunks = [s_bufs[slot][:, c * 128:(c + 1) * 128] for c in range(nch)]
        if diagonal:
            rows = lax.broadcasted_iota(jnp.int32, (blk, 128), 0)
            cols = lax.broadcasted_iota(jnp.int32, (blk, 128), 1)
            chunks = [jnp.where(cols + c * 128 <= rows, ch, -jnp.inf) for c, ch in enumerate(chunks)]
        mx = functools.reduce(jnp.maximum, chunks)
        m_prev = m_sc[...]
        m_new = jnp.maximum(m_prev, jnp.max(mx, axis=-1, keepdims=True))
        alpha = jnp.exp2((m_prev - m_new) * log2_scale)
        psum = None
        for c, ch in enumerate(chunks):
            p = jnp.exp2((ch - m_new) * log2_scale)
            p_bufs[slot][:, c * 128:(c + 1) * 128] = p.astype(BF16)
            psum = p if psum is None else psum + p
        l_sc[...] = alpha * l_sc[...] + psum
        al_bufs[slot][...] = alpha
        m_sc[...] = m_new

    def pv(j, slot):
        r0 = pl.multiple_of(jnp.maximum(j, 0) * blk, blk)
        acc_sc[...] = (al_bufs[slot][...] * acc_sc[...]
                       + _dot(p_bufs[slot][...], v_ref[pl.ds(r0, blk), :]))

    def half_step(j, a, diagonal=False, prefetch=True):
        if prefetch:
            scores(j + 1, 1 - a)
        softmax(a, diagonal)
        pv(j - 1, 1 - a)

    scores(0, 0)

    def pair(jj, carry):
        half_step(2 * jj, 0)
        half_step(2 * jj + 1, 1)
        return carry

    lax.fori_loop(0, i // 2, pair, 0)

    @pl.when(i % 2 == 0)
    def _():
        half_step(i, 0, diagonal=True, prefetch=False)
        pv(i, 0)

    @pl.when(i % 2 == 1)
    def _():
        half_step(i - 1, 0)
        half_step(i, 1, diagonal=True, prefetch=False)
        pv(i, 1)

    o = acc_sc[...] / jnp.sum(l_sc[...], axis=-1, keepdims=True)
    o_ref[...] = _rms(o, g_ref[...]).astype(BF16)


def _mla_attn(q, k, v, out_norm_g, t, blk):
    scale = (MLA_NOPE + MLA_ROPE) ** -0.5
    return pl.pallas_call(
        functools.partial(_attn_kernel, blk=blk, log2_scale=scale * 1.4426950408889634),
        grid=(MLA_HEADS, t // blk),
        in_specs=[pl.BlockSpec((blk, MLA_QK_PAD), lambda h, i: (i, h)),
                  pl.BlockSpec((t, MLA_QK_PAD), lambda h, i: (0, h)),
                  pl.BlockSpec((t, MLA_V), lambda h, i: (0, h)),
                  pl.BlockSpec((1, MLA_V), lambda h, i: (0, h))],
        out_specs=pl.BlockSpec((blk, MLA_V), lambda h, i: (i, h)),
        out_shape=jax.ShapeDtypeStruct((t, MLA_HEADS * MLA_V), BF16),
        scratch_shapes=[pltpu.VMEM((blk, blk), F32), pltpu.VMEM((blk, blk), F32),
                        pltpu.VMEM((blk, blk), BF16), pltpu.VMEM((blk, blk), BF16),
                        pltpu.VMEM((blk, 128), F32), pltpu.VMEM((blk, 128), F32),
                        pltpu.VMEM((blk, 128), F32), pltpu.VMEM((blk, 128), F32),
                        pltpu.VMEM((blk, MLA_V), F32)],
        compiler_params=pltpu.CompilerParams(dimension_semantics=("arbitrary", "arbitrary"),
                                             vmem_limit_bytes=V7X_VMEM_LIMIT_BYTES),
        name="mla_attn",
    )(q, k, v, out_norm_g)


def _out_ffn_kernel(x_ref, ycg_ref, ymla_ref, wout_ref, fg_ref, wgate_ref, wup_ref, wdown_ref,
                    og_ref, o_ref, *, final_norm):
    half = CONV_CH + GLA_VW
    x1 = (x_ref[...] + _dot(ycg_ref[...], wout_ref[0:half, :])
          + _dot(ymla_ref[...], wout_ref[half:2 * half, :]))
    h = _rms(x1, fg_ref[...]).astype(BF16)
    g = _dot(h, wgate_ref[...])
    u = _dot(h, wup_ref[...])
    a = (g * _sigmoid(g) * u).astype(BF16)
    out = x1 + _dot(a, wdown_ref[...])
    if final_norm:
        out = _rms(out, og_ref[...])
    o_ref[...] = out


def _out_ffn(x, ycg, ymla, lw, final_g, t, tm, final_norm):
    row = lambda w: pl.BlockSpec((tm, w), lambda i: (i, 0))
    resident = pl.BlockSpec(memory_space=pltpu.VMEM)
    return pl.pallas_call(
        functools.partial(_out_ffn_kernel, final_norm=final_norm),
        grid=(t // tm,),
        in_specs=[row(D_MODEL), row(CONV_CH + GLA_VW), row(MLA_HEADS * MLA_V),
                  resident, resident, resident, resident, resident, resident],
        out_specs=row(D_MODEL),
        out_shape=jax.ShapeDtypeStruct((t, D_MODEL), F32),
        compiler_params=pltpu.CompilerParams(dimension_semantics=("arbitrary",),
                                             vmem_limit_bytes=V7X_VMEM_LIMIT_BYTES),
        name="out_ffn",
    )(x, ycg, ymla, lw["w_out"], lw["ffn_norm_g"], lw["w_gate"], lw["w_up"], lw["w_down"], final_g)


def _pack_w_in(w):
    glow = jnp.pad(w[..., 1280:1296], ((0, 0), (0, 0), (0, 128 - GLA_GATE_RANK)))
    return jnp.concatenate(
        [w[..., 0:768], w[..., 768:1024], w[..., 1024:1280], w[..., 1296:1552], glow,
         w[..., 1552:1808], w[..., 1808:1936], w[..., 1936:2000], w[..., 1968:2000],
         w[..., 1936:1968]], axis=-1).astype(BF16)


def _pack_w_uq(w):
    depth = w.shape[0]
    w = w.reshape(depth, MLA_Q_RANK, MLA_HEADS, MLA_NOPE + MLA_ROPE)
    nope, r1, r2 = w[..., :MLA_NOPE], w[..., MLA_NOPE:MLA_NOPE + 32], w[..., MLA_NOPE + 32:]
    return jnp.concatenate([nope, r1, r2, r2, r1], axis=-1).reshape(
        depth, MLA_Q_RANK, MLA_HEADS * MLA_QK_PAD).astype(BF16)


def kernel(x, positions, attn_norm_g, w_in, conv_w, conv_norm_g, gla_w_gate, gla_b_gate, gla_norm_g, mla_q_norm_g, mla_w_uq, mla_kv_norm_g, mla_w_ukv, mla_out_norm_g, w_out, ffn_norm_g, w_gate, w_up, w_down, final_norm_g):
    batch, t, _ = x.shape
    assert batch == 1
    depth = w_in.shape[0]
    tm = min(t, 512)
    blk = min(t, 512)
    assert t % tm == 0 and t % blk == 0 and tm % 128 == 0

    row = lambda a: a.reshape(depth, 1, a.shape[-1])
    packed = {
        "attn_norm_g": row(attn_norm_g),
        "w_in": _pack_w_in(w_in),
        "conv_w": jnp.pad(conv_w, ((0, 0), (0, 8 - conv_w.shape[1]), (0, 0))),
        "conv_norm_g": row(conv_norm_g),
        "gla_w_gate": jnp.pad(gla_w_gate, ((0, 0), (0, 128 - GLA_GATE_RANK), (0, 0))).astype(BF16),
        "gla_b_gate": row(gla_b_gate),
        "gla_norm_g": row(gla_norm_g),
        "mla_q_norm_g": row(mla_q_norm_g),
        "mla_w_uq": _pack_w_uq(mla_w_uq),
        "mla_kv_norm_g": row(mla_kv_norm_g),
        "mla_w_ukv": mla_w_ukv.astype(BF16),
        "mla_out_norm_g": row(mla_out_norm_g),
        "w_out": w_out.astype(BF16),
        "ffn_norm_g": row(ffn_norm_g),
        "w_gate": w_gate.astype(BF16),
        "w_up": w_up.astype(BF16),
        "w_down": w_down.astype(BF16),
    }
    final_g = final_norm_g.reshape(1, D_MODEL)

    tab = _rope_table(positions, t)
    xs = x.reshape(t, D_MODEL)
    for l in range(depth):
        lw = {name: a[l] for name, a in packed.items()}
        ycg, q, k, v = _mixer_in(xs, tab, lw, t, tm)
        ymla = _mla_attn(q, k, v, lw["mla_out_norm_g"], t, blk)
        xs = _out_ffn(xs, ycg, ymla, lw, final_g, t, tm, final_norm=(l == depth - 1))
    return xs.reshape(batch, t, D_MODEL)
```

```python
import functools

import jax
import jax.numpy as jnp
from jax import lax
from jax.experimental import pallas as pl
from jax.experimental.pallas import tpu as pltpu

D_MODEL = 1024
CONV_CH = 256
CONV_GROUP_W = 64
GLA_HEADS = 4
GLA_DK = 32
GLA_DV = 64
GLA_GATE_RANK = 16
GLA_TAU = 16.0
GLA_QK = GLA_HEADS * GLA_DK
GLA_VW = GLA_HEADS * GLA_DV
GLA_BLOCK = 16
MLA_HEADS = 4
MLA_Q_RANK = 256
MLA_KV_RANK = 128
MLA_NOPE = 128
MLA_ROPE = 64
MLA_V = 128
MLA_QK_PAD = 256
ATTN_ROWS = 32
ROPE_THETA = 10000.0
D_FF = 2816
EPS = 1e-6
ATTN_LOG2_SCALE = (MLA_NOPE + MLA_ROPE) ** -0.5 * 1.4426950408889634

C_CONV = 0
C_QK = 768
C_GV = 1024
C_GR = 1280
C_GLOW = 1536
C_CQ = 1664
C_CKV = 1920
C_KR = 2048
D_IN_PACKED = 2176

V7X_VMEM_LIMIT_BYTES = 58 * 1024 * 1024

F32 = jnp.float32
BF16 = jnp.bfloat16
NEG_BIG = -1e30


def _dot(a, b):
    return jnp.dot(a, b, preferred_element_type=F32)


def _dot_nt(a, b):
    return lax.dot_general(a, b, (((1,), (1,)), ((), ())), preferred_element_type=F32)


def _dot_tn(a, b):
    return lax.dot_general(a, b, (((0,), (0,)), ((), ())), preferred_element_type=F32)


def _split3(x):
    hi = x.astype(BF16)
    r = x - hi.astype(F32)
    mid = r.astype(BF16)
    lo = (r - mid.astype(F32)).astype(BF16)
    return hi, mid, lo


def _rms(x, g):
    return x * lax.rsqrt(jnp.mean(x * x, axis=-1, keepdims=True) + EPS) * g


def _group_indicator(n, width):
    r = lax.broadcasted_iota(jnp.int32, (n, n), 0) // width
    c = lax.broadcasted_iota(jnp.int32, (n, n), 1) // width
    return (r == c).astype(BF16)


def _group_rms(x, ind, width):
    sq = x * x
    hi = sq.astype(BF16)
    lo = (sq - hi.astype(F32)).astype(BF16)
    ms = (_dot(hi, ind) + _dot(lo, ind)) * (1.0 / width)
    return x * lax.rsqrt(ms + EPS)


def _sigmoid(x):
    return 1.0 / (1.0 + jnp.exp(-x))


def _rope_kernel(pos_ref, inv_ref, tab_ref):
    ang = pos_ref[...].astype(F32) * inv_ref[...]
    lane = lax.broadcasted_iota(jnp.int32, ang.shape, 1)
    c = jnp.cos(ang)
    s = jnp.sin(ang)
    tab_ref[...] = jnp.where(lane < 64, c, jnp.where(lane < 96, -s, s))


def _rope_table(positions, t):
    tr = min(t, 1024)
    inv = 1.0 / (ROPE_THETA ** (jnp.arange(0, MLA_ROPE, 2, dtype=F32) / MLA_ROPE))
    inv4 = jnp.tile(inv, 4).reshape(1, 128)
    return pl.pallas_call(
        _rope_kernel,
        grid=(t // tr,),
        in_specs=[pl.BlockSpec((tr, 1), lambda i: (i, 0)),
                  pl.BlockSpec((1, 128), lambda i: (0, 0))],
        out_specs=pl.BlockSpec((tr, 128), lambda i: (i, 0)),
        out_shape=jax.ShapeDtypeStruct((t, 128), F32),
        name="rope_table",
    )(positions.reshape(t, 1), inv4)


def _mixer_kernel(x_ref, tab_ref, ang_ref, win_ref, convw_ref, convg_ref, wg_ref, bg_ref, glag_ref,
                  qg_ref, wuq_ref, kvg_ref, wukv_ref,
                  ycg_ref, q_ref, k_ref, v_ref,
                  vbuf, st_ref, qs_ref, ks_ref, vs_ref, dec_ref, oi_ref, *, tm):
    nb = tm // GLA_BLOCK

    @pl.when(pl.program_id(0) == 0)
    def _():
        vbuf[0:8, :] = jnp.zeros((8, CONV_CH), F32)
        st_ref[...] = jnp.zeros(st_ref.shape, F32)

    h = _rms(x_ref[...], ang_ref[...]).astype(BF16)

    def proj(c0, n):
        return _dot(h, win_ref[:, c0:c0 + n])

    ind64 = _group_indicator(256, 64)

    zc = proj(C_CONV, 3 * CONV_CH)
    cb = zc[:, 0:CONV_CH]
    v = zc[:, CONV_CH:2 * CONV_CH] * zc[:, 2 * CONV_CH:3 * CONV_CH]
    vbuf[8:8 + tm, :] = v
    y = (convw_ref[2:3, :] * v + convw_ref[1:2, :] * vbuf[7:7 + tm, :]
         + convw_ref[0:1, :] * vbuf[6:6 + tm, :])
    vbuf[0:8, :] = vbuf[tm:tm + 8, :]
    y_conv = _group_rms(cb * y, ind64, CONV_GROUP_W) * convg_ref[...]
    ycg_ref[:, 0:CONV_CH] = y_conv.astype(BF16)

    zqk = proj(C_QK, 2 * GLA_QK)
    gq = zqk[:, 0:GLA_QK] * (GLA_DK ** -0.5)
    gk = zqk[:, GLA_QK:2 * GLA_QK]
    gv = proj(C_GV, GLA_VW)
    glow = proj(C_GLOW, 128)
    xg = _dot(glow.astype(BF16), wg_ref[...]) + bg_ref[...]
    la = (jnp.minimum(xg, 0.0) - jnp.log(1.0 + jnp.exp(-jnp.abs(xg)))) * (1.0 / GLA_TAU)

    rr = lax.broadcasted_iota(jnp.int32, (128, 128), 0)
    cc = lax.broadcasted_iota(jnp.int32, (128, 128), 1)
    same = (rr // GLA_BLOCK) == (cc // GLA_BLOCK)
    tri = (same & (cc <= rr)).astype(BF16)
    blk = same.astype(BF16)
    parts = _split3(la)
    b_rows, e_rows = [], []
    for g in range(tm // 128):
        sl = slice(g * 128, (g + 1) * 128)
        b_rows.append(_dot(tri, parts[0][sl]) + _dot(tri, parts[1][sl]) + _dot(tri, parts[2][sl]))
        e_rows.append(_dot(blk, parts[0][sl]) + _dot(blk, parts[1][sl]) + _dot(blk, parts[2][sl]))
    b = jnp.concatenate(b_rows, axis=0)
    bend = jnp.concatenate(e_rows, axis=0)

    hr = lax.broadcasted_iota(jnp.int32, (GLA_QK, GLA_VW), 0) // GLA_DK
    hc = lax.broadcasted_iota(jnp.int32, (GLA_QK, GLA_VW), 1) // GLA_DV
    expand = (hr == hc).astype(BF16)
    tpos = lax.broadcasted_iota(jnp.int32, (tm, GLA_QK), 0) % GLA_BLOCK
    od = _dot((gq * gk).astype(BF16), expand) * gv
    for d in range(1, GLA_BLOCK):
        arg = jnp.where(tpos >= d, b - pltpu.roll(b, d, axis=0), NEG_BIG)
        p = gq * jnp.exp(arg) * pltpu.roll(gk, d, axis=0)
        od = od + _dot(p.astype(BF16), expand) * pltpu.roll(gv, d, axis=0)

    qs_ref[...] = (gq * jnp.exp(b)).astype(BF16)
    ks_ref[...] = (gk * jnp.exp(bend - b)).astype(BF16)
    vs_ref[...] = gv.astype(BF16)
    dec_ref[...] = jnp.exp(bend)
    mr = lax.broadcasted_iota(jnp.int32, (GLA_VW, GLA_QK), 0) // GLA_DV
    mc = lax.broadcasted_iota(jnp.int32, (GLA_VW, GLA_QK), 1) // GLA_DK
    head_mask = (mr == mc).astype(F32)

    def block_step(j, st):
        r0 = pl.multiple_of(j * GLA_BLOCK, GLA_BLOCK)
        qj = qs_ref[pl.ds(r0, GLA_BLOCK), :]
        kj = ks_ref[pl.ds(r0, GLA_BLOCK), :]
        vj = vs_ref[pl.ds(r0, GLA_BLOCK), :]
        dj = dec_ref[pl.ds(r0, 1), :]
        oi_ref[pl.ds(r0, GLA_BLOCK), :] = _dot_nt(qj, st.astype(BF16))
        return st * dj + _dot_tn(vj, kj) * head_mask

    st_ref[...] = lax.fori_loop(0, nb, block_step, st_ref[...], unroll=8)

    gr = proj(C_GR, GLA_VW)
    o = oi_ref[...] + od
    y_gla = _group_rms(o, ind64, GLA_DV) * glag_ref[...] * (gr * _sigmoid(gr))
    ycg_ref[:, CONV_CH:CONV_CH + GLA_VW] = y_gla.astype(BF16)

    tab = tab_ref[...]
    qtab = tab * ATTN_LOG2_SCALE
    qa = _dot(_rms(proj(C_CQ, MLA_Q_RANK), qg_ref[...]).astype(BF16), wuq_ref[...])
    kva = _dot(_rms(proj(C_CKV, MLA_KV_RANK), kvg_ref[...]).astype(BF16), wukv_ref[...])
    kr = proj(C_KR, 128) * tab
    lane = lax.broadcasted_iota(jnp.int32, (tm, 128), 1)
    kro = jnp.where(lane < MLA_ROPE, kr + pltpu.roll(kr, 64, axis=1), 0.0).astype(BF16)
    for hd in range(MLA_HEADS):
        c0 = hd * MLA_QK_PAD
        q_ref[:, c0:c0 + 128] = (qa[:, c0:c0 + 128] * ATTN_LOG2_SCALE).astype(BF16)
        qr = qa[:, c0 + 128:c0 + 256] * qtab
        q_ref[:, c0 + 128:c0 + 256] = (qr + pltpu.roll(qr, 64, axis=1)).astype(BF16)
        k_ref[:, c0:c0 + 128] = kva[:, c0:c0 + 128].astype(BF16)
        k_ref[:, c0 + 128:c0 + 256] = kro
        v_ref[:, hd * MLA_V:(hd + 1) * MLA_V] = kva[:, c0 + 128:c0 + 256].astype(BF16)


def _mixer_in(x, tab, lw, t, tm):
    row = lambda w: pl.BlockSpec((tm, w), lambda i: (i, 0))
    full = lambda a: pl.BlockSpec(a.shape, lambda i: (0,) * a.ndim)
    consts = [lw["attn_norm_g"], lw["w_in"], lw["conv_w"], lw["conv_norm_g"], lw["gla_w_gate"],
              lw["gla_b_gate"], lw["gla_norm_g"], lw["mla_q_norm_g"], lw["mla_w_uq"],
              lw["mla_kv_norm_g"], lw["mla_w_ukv"]]
    return pl.pallas_call(
        functools.partial(_mixer_kernel, tm=tm),
        grid=(t // tm,),
        in_specs=[row(D_MODEL), row(128)] + [full(a) for a in consts],
        out_specs=[row(CONV_CH + GLA_VW), row(MLA_HEADS * MLA_QK_PAD), row(MLA_HEADS * MLA_QK_PAD),
                   row(MLA_HEADS * MLA_V)],
        out_shape=[jax.ShapeDtypeStruct((t, CONV_CH + GLA_VW), BF16),
                   jax.ShapeDtypeStruct((t, MLA_HEADS * MLA_QK_PAD), BF16),
                   jax.ShapeDtypeStruct((t, MLA_HEADS * MLA_QK_PAD), BF16),
                   jax.ShapeDtypeStruct((t, MLA_HEADS * MLA_V), BF16)],
        scratch_shapes=[pltpu.VMEM((tm + 8, CONV_CH), F32),
                        pltpu.VMEM((GLA_VW, GLA_QK), F32),
                        pltpu.VMEM((tm, GLA_QK), BF16),
                        pltpu.VMEM((tm, GLA_QK), BF16),
                        pltpu.VMEM((tm, GLA_VW), BF16),
                        pltpu.VMEM((tm, GLA_QK), F32),
                        pltpu.VMEM((tm, GLA_VW), F32)],
        compiler_params=pltpu.CompilerParams(dimension_semantics=("arbitrary",),
                                             vmem_limit_bytes=V7X_VMEM_LIMIT_BYTES),
        name="mixer_in",
    )(x, tab, *consts)


def _attn_kernel(q_ref, k_ref, v_ref, g_ref, o_ref, s0, s1, p0, p1, al0, al1, m_sc, l_sc, acc_sc,
                 *, blk):
    i = pl.program_id(1)
    nch = blk // 128
    s_bufs, p_bufs, al_bufs = (s0, s1), (p0, p1), (al0, al1)
    m_sc[...] = jnp.full(m_sc.shape, -jnp.inf, F32)
    l_sc[...] = jnp.zeros(l_sc.shape, F32)
    acc_sc[...] = jnp.zeros(acc_sc.shape, F32)
    p1[...] = jnp.zeros(p1.shape, BF16)
    al1[...] = jnp.ones(al1.shape, F32)

    def scores(j, slot):
        r0 = pl.multiple_of(j * blk, blk)
        s_bufs[slot][...] = _dot_nt(q_ref[...], k_ref[pl.ds(r0, blk), :])

    def softmax(slot, diagonal):
        s_ref, p_ref, al_ref = s_bufs[slot], p_bufs[slot], al_bufs[slot]
        for r in range(blk // ATTN_ROWS):
            rs = slice(r * ATTN_ROWS, (r + 1) * ATTN_ROWS)
            live = min(nch, ((r + 1) * ATTN_ROWS - 1) // 128 + 1) if diagonal else nch
            need_mask = [diagonal and (c + 1) * 128 - 1 > r * ATTN_ROWS for c in range(live)]
            if any(need_mask):
                rows = lax.broadcasted_iota(jnp.int32, (ATTN_ROWS, 128), 0) + r * ATTN_ROWS
                cols = lax.broadcasted_iota(jnp.int32, (ATTN_ROWS, 128), 1)

            def load(c):
                ch = s_ref[rs, c * 128:(c + 1) * 128]
                if need_mask[c]:
                    ch = jnp.where(cols + c * 128 <= rows, ch, -jnp.inf)
                return ch

            mx = functools.reduce(jnp.maximum, [load(c) for c in range(live)])
            m_prev = m_sc[rs, :]
            m_new = jnp.maximum(m_prev, jnp.max(mx, axis=-1, keepdims=True))
            alpha = jnp.exp2(m_prev - m_new)
            m_sc[rs, :] = m_new
            al_ref[rs, :] = alpha
            psum = None
            for c in range(nch):
                if c < live:
                    p = jnp.exp2(load(c) - m_new)
                    p_ref[rs, c * 128:(c + 1) * 128] = p.astype(BF16)
                    psum = p if psum is None else psum + p
                else:
                    p_ref[rs, c * 128:(c + 1) * 128] = jnp.zeros((ATTN_ROWS, 128), BF16)
            l_sc[rs, :] = alpha * l_sc[rs, :] + psum

    def pv(j, slot):
        r0 = pl.multiple_of(jnp.maximum(j, 0) * blk, blk)
        acc_sc[...] = (al_bufs[slot][...] * acc_sc[...]
                       + _dot(p_bufs[slot][...], v_ref[pl.ds(r0, blk), :]))

    def half_step(j, a, diagonal=False, prefetch=True):
        if prefetch:
            scores(j + 1, 1 - a)
        softmax(a, diagonal)
        pv(j - 1, 1 - a)

    scores(0, 0)

    def pair(jj, carry):
        half_step(2 * jj, 0)
        half_step(2 * jj + 1, 1)
        return carry

    lax.fori_loop(0, i // 2, pair, 0)

    @pl.when(i % 2 == 0)
    def _():
        half_step(i, 0, diagonal=True, prefetch=False)
        pv(i, 0)

    @pl.when(i % 2 == 1)
    def _():
        half_step(i - 1, 0)
        half_step(i, 1, diagonal=True, prefetch=False)
        pv(i, 1)

    o = acc_sc[...] / jnp.sum(l_sc[...], axis=-1, keepdims=True)
    o_ref[...] = _rms(o, g_ref[...]).astype(BF16)


def _mla_attn(q, k, v, out_norm_g, t, blk):
    return pl.pallas_call(
        functools.partial(_attn_kernel, blk=blk),
        grid=(MLA_HEADS, t // blk),
        in_specs=[pl.BlockSpec((blk, MLA_QK_PAD), lambda h, i: (i, h)),
                  pl.BlockSpec((t, MLA_QK_PAD), lambda h, i: (0, h)),
                  pl.BlockSpec((t, MLA_V), lambda h, i: (0, h)),
                  pl.BlockSpec((1, MLA_V), lambda h, i: (0, h))],
        out_specs=pl.BlockSpec((blk, MLA_V), lambda h, i: (i, h)),
        out_shape=jax.ShapeDtypeStruct((t, MLA_HEADS * MLA_V), BF16),
        scratch_shapes=[pltpu.VMEM((blk, blk), F32), pltpu.VMEM((blk, blk), F32),
                        pltpu.VMEM((blk, blk), BF16), pltpu.VMEM((blk, blk), BF16),
                        pltpu.VMEM((blk, 128), F32), pltpu.VMEM((blk, 128), F32),
                        pltpu.VMEM((blk, 128), F32), pltpu.VMEM((blk, 128), F32),
                        pltpu.VMEM((blk, MLA_V), F32)],
        compiler_params=pltpu.CompilerParams(dimension_semantics=("arbitrary", "arbitrary"),
                                             vmem_limit_bytes=V7X_VMEM_LIMIT_BYTES),
        name="mla_attn",
    )(q, k, v, out_norm_g)


def _out_ffn_kernel(x_ref, ycg_ref, ymla_ref, wout_ref, fg_ref, wgate_ref, wup_ref, wdown_ref,
                    og_ref, o_ref, *, final_norm):
    half = CONV_CH + GLA_VW
    x1 = (x_ref[...] + _dot(ycg_ref[...], wout_ref[0:half, :])
          + _dot(ymla_ref[...], wout_ref[half:2 * half, :]))
    h = _rms(x1, fg_ref[...]).astype(BF16)
    g = _dot(h, wgate_ref[...])
    u = _dot(h, wup_ref[...])
    a = (g * _sigmoid(g) * u).astype(BF16)
    out = x1 + _dot(a, wdown_ref[...])
    if final_norm:
        out = _rms(out, og_ref[...])
    o_ref[...] = out


def _out_ffn(x, ycg, ymla, lw, final_g, t, tm, final_norm):
    row = lambda w: pl.BlockSpec((tm, w), lambda i: (i, 0))
    resident = pl.BlockSpec(memory_space=pltpu.VMEM)
    return pl.pallas_call(
        functools.partial(_out_ffn_kernel, final_norm=final_norm),
        grid=(t // tm,),
        in_specs=[row(D_MODEL), row(CONV_CH + GLA_VW), row(MLA_HEADS * MLA_V),
                  resident, resident, resident, resident, resident, resident],
        out_specs=row(D_MODEL),
        out_shape=jax.ShapeDtypeStruct((t, D_MODEL), F32),
        compiler_params=pltpu.CompilerParams(dimension_semantics=("arbitrary",),
                                             vmem_limit_bytes=V7X_VMEM_LIMIT_BYTES),
        name="out_ffn",
    )(x, ycg, ymla, lw["w_out"], lw["ffn_norm_g"], lw["w_gate"], lw["w_up"], lw["w_down"], final_g)


def _pack_w_in(w):
    glow = jnp.pad(w[..., 1280:1296], ((0, 0), (0, 0), (0, 128 - GLA_GATE_RANK)))
    return jnp.concatenate(
        [w[..., 0:768], w[..., 768:1024], w[..., 1024:1280], w[..., 1296:1552], glow,
         w[..., 1552:1808], w[..., 1808:1936], w[..., 1936:2000], w[..., 1968:2000],
         w[..., 1936:1968]], axis=-1).astype(BF16)


def _pack_w_uq(w):
    depth = w.shape[0]
    w = w.reshape(depth, MLA_Q_RANK, MLA_HEADS, MLA_NOPE + MLA_ROPE)
    nope, r1, r2 = w[..., :MLA_NOPE], w[..., MLA_NOPE:MLA_NOPE + 32], w[..., MLA_NOPE + 32:]
    return jnp.concatenate([nope, r1, r2, r2, r1], axis=-1).reshape(
        depth, MLA_Q_RANK, MLA_HEADS * MLA_QK_PAD).astype(BF16)


def kernel(x, positions, attn_norm_g, w_in, conv_w, conv_norm_g, gla_w_gate, gla_b_gate, gla_norm_g, mla_q_norm_g, mla_w_uq, mla_kv_norm_g, mla_w_ukv, mla_out_norm_g, w_out, ffn_norm_g, w_gate, w_up, w_down, final_norm_g):
    batch, t, _ = x.shape
    assert batch == 1
    depth = w_in.shape[0]
    tm = min(t, 512)
    blk = min(t, 1024)
    assert t % tm == 0 and t % blk == 0 and tm % 128 == 0

    row = lambda a: a.reshape(depth, 1, a.shape[-1])
    packed = {
        "attn_norm_g": row(attn_norm_g),
        "w_in": _pack_w_in(w_in),
        "conv_w": jnp.pad(conv_w, ((0, 0), (0, 8 - conv_w.shape[1]), (0, 0))),
        "conv_norm_g": row(conv_norm_g),
        "gla_w_gate": jnp.pad(gla_w_gate, ((0, 0), (0, 128 - GLA_GATE_RANK), (0, 0))).astype(BF16),
        "gla_b_gate": row(gla_b_gate),
        "gla_norm_g": row(gla_norm_g),
        "mla_q_norm_g": row(mla_q_norm_g),
        "mla_w_uq": _pack_w_uq(mla_w_uq),
        "mla_kv_norm_g": row(mla_kv_norm_g),
        "mla_w_ukv": mla_w_ukv.astype(BF16),
        "mla_out_norm_g": row(mla_out_norm_g),
        "w_out": w_out.astype(BF16),
        "ffn_norm_g": row(ffn_norm_g),
        "w_gate": w_gate.astype(BF16),
        "w_up": w_up.astype(BF16),
        "w_down": w_down.astype(BF16),
    }
    final_g = final_norm_g.reshape(1, D_MODEL)

    tab = _rope_table(positions, t)
    xs = x.reshape(t, D_MODEL)
    for l in range(depth):
        lw = {name: a[l] for name, a in packed.items()}
        ycg, q, k, v = _mixer_in(xs, tab, lw, t, tm)
        ymla = _mla_attn(q, k, v, lw["mla_out_norm_g"], t, blk)
        xs = _out_ffn(xs, ycg, ymla, lw, final_g, t, tm, final_norm=(l == depth - 1))
    return xs.reshape(batch, t, D_MODEL)
```

```python
import functools

import jax
import jax.numpy as jnp
from jax import lax
from jax.experimental import pallas as pl
from jax.experimental.pallas import tpu as pltpu

D_MODEL = 1024
CONV_CH = 256
CONV_GROUP_W = 64
GLA_HEADS = 4
GLA_DK = 32
GLA_DV = 64
GLA_GATE_RANK = 16
GLA_TAU = 16.0
GLA_QK = GLA_HEADS * GLA_DK
GLA_VW = GLA_HEADS * GLA_DV
GLA_BLOCK = 16
MLA_HEADS = 4
MLA_Q_RANK = 256
MLA_KV_RANK = 128
MLA_NOPE = 128
MLA_ROPE = 64
MLA_V = 128
MLA_QK_PAD = 256
ATTN_ROWS = 32
ROPE_THETA = 10000.0
D_FF = 2816
EPS = 1e-6
ATTN_LOG2_SCALE = (MLA_NOPE + MLA_ROPE) ** -0.5 * 1.4426950408889634

C_CONV = 0
C_QK = 768
C_GV = 1024
C_GR = 1280
C_GLOW = 1536
C_CQ = 1664
C_CKV = 1920
C_KR = 2048
D_IN_PACKED = 2176

V7X_VMEM_LIMIT_BYTES = 58 * 1024 * 1024

F32 = jnp.float32
BF16 = jnp.bfloat16
NEG_BIG = -1e30


def _dot(a, b):
    return jnp.dot(a, b, preferred_element_type=F32)


def _dot_nt(a, b):
    return lax.dot_general(a, b, (((1,), (1,)), ((), ())), preferred_element_type=F32)


def _dot_tn(a, b):
    return lax.dot_general(a, b, (((0,), (0,)), ((), ())), preferred_element_type=F32)


def _split3(x):
    hi = x.astype(BF16)
    r = x - hi.astype(F32)
    mid = r.astype(BF16)
    lo = (r - mid.astype(F32)).astype(BF16)
    return hi, mid, lo


def _rms(x, g):
    return x * lax.rsqrt(jnp.mean(x * x, axis=-1, keepdims=True) + EPS) * g


def _group_indicator(n, width):
    r = lax.broadcasted_iota(jnp.int32, (n, n), 0) // width
    c = lax.broadcasted_iota(jnp.int32, (n, n), 1) // width
    return (r == c).astype(BF16)


def _group_rms(x, ind, width):
    sq = x * x
    hi = sq.astype(BF16)
    lo = (sq - hi.astype(F32)).astype(BF16)
    ms = (_dot(hi, ind) + _dot(lo, ind)) * (1.0 / width)
    return x * lax.rsqrt(ms + EPS)


def _sigmoid(x):
    return 1.0 / (1.0 + jnp.exp(-x))


def _rope_kernel(pos_ref, inv_ref, tab_ref):
    ang = pos_ref[...].astype(F32) * inv_ref[...]
    lane = lax.broadcasted_iota(jnp.int32, ang.shape, 1)
    c = jnp.cos(ang)
    s = jnp.sin(ang)
    tab_ref[...] = jnp.where(lane < 64, c, jnp.where(lane < 96, -s, s))


def _rope_table(positions, t):
    tr = min(t, 1024)
    inv = 1.0 / (ROPE_THETA ** (jnp.arange(0, MLA_ROPE, 2, dtype=F32) / MLA_ROPE))
    inv4 = jnp.tile(inv, 4).reshape(1, 128)
    return pl.pallas_call(
        _rope_kernel,
        grid=(t // tr,),
        in_specs=[pl.BlockSpec((tr, 1), lambda i: (i, 0)),
                  pl.BlockSpec((1, 128), lambda i: (0, 0))],
        out_specs=pl.BlockSpec((tr, 128), lambda i: (i, 0)),
        out_shape=jax.ShapeDtypeStruct((t, 128), F32),
        name="rope_table",
    )(positions.reshape(t, 1), inv4)


def _mixer_kernel(x_ref, tab_ref, ang_ref, win_ref, convw_ref, convg_ref, wg_ref, bg_ref, glag_ref,
                  qg_ref, wuq_ref, kvg_ref, wukv_ref,
                  ycg_ref, q_ref, k_ref, v_ref,
                  vbuf, st_ref, ut_ref, sb_ref, *, tm):
    nb = tm // GLA_BLOCK

    @pl.when(pl.program_id(0) == 0)
    def _():
        vbuf[0:8, :] = jnp.zeros((8, CONV_CH), F32)
        st_ref[...] = jnp.zeros(st_ref.shape, F32)

    h = _rms(x_ref[...], ang_ref[...]).astype(BF16)

    def proj(c0, n):
        return _dot(h, win_ref[:, c0:c0 + n])

    ind64 = _group_indicator(256, 64)

    zc = proj(C_CONV, 3 * CONV_CH)
    cb = zc[:, 0:CONV_CH]
    v = zc[:, CONV_CH:2 * CONV_CH] * zc[:, 2 * CONV_CH:3 * CONV_CH]
    vbuf[8:8 + tm, :] = v
    y = (convw_ref[2:3, :] * v + convw_ref[1:2, :] * vbuf[7:7 + tm, :]
         + convw_ref[0:1, :] * vbuf[6:6 + tm, :])
    vbuf[0:8, :] = vbuf[tm:tm + 8, :]
    y_conv = _group_rms(cb * y, ind64, CONV_GROUP_W) * convg_ref[...]
    ycg_ref[:, 0:CONV_CH] = y_conv.astype(BF16)

    zqk = proj(C_QK, 2 * GLA_QK)
    gq = zqk[:, 0:GLA_QK] * (GLA_DK ** -0.5)
    gk = zqk[:, GLA_QK:2 * GLA_QK]
    gv = proj(C_GV, GLA_VW)
    glow = proj(C_GLOW, 128)
    xg = _dot(glow.astype(BF16), wg_ref[...]) + bg_ref[...]
    la = (jnp.minimum(xg, 0.0) - jnp.log(1.0 + jnp.exp(-jnp.abs(xg)))) * (1.0 / GLA_TAU)

    rr = lax.broadcasted_iota(jnp.int32, (128, 128), 0)
    cc = lax.broadcasted_iota(jnp.int32, (128, 128), 1)
    same = (rr // GLA_BLOCK) == (cc // GLA_BLOCK)
    tri = (same & (cc <= rr)).astype(BF16)
    blk = same.astype(BF16)
    cat = jnp.concatenate(_split3(la), axis=1)
    tri_blk = jnp.concatenate([tri, blk], axis=0)
    b_rows, e_rows = [], []
    for g in range(tm // 128):
        r3 = _dot(tri_blk, cat[g * 128:(g + 1) * 128])
        r = r3[:, 0:128] + r3[:, 128:256] + r3[:, 256:384]
        b_rows.append(r[0:128])
        e_rows.append(r[128:256])
    b = jnp.concatenate(b_rows, axis=0)
    bend = jnp.concatenate(e_rows, axis=0)

    hr = lax.broadcasted_iota(jnp.int32, (GLA_QK, GLA_VW), 0) // GLA_DK
    hc = lax.broadcasted_iota(jnp.int32, (GLA_QK, GLA_VW), 1) // GLA_DV
    expand = (hr == hc).astype(BF16)
    tpos = lax.broadcasted_iota(jnp.int32, (tm, GLA_QK), 0) % GLA_BLOCK
    od = _dot((gq * gk).astype(BF16), expand) * gv
    for d in range(1, GLA_BLOCK):
        arg = jnp.where(tpos >= d, b - pltpu.roll(b, d, axis=0), NEG_BIG)
        p = gq * jnp.exp(arg) * pltpu.roll(gk, d, axis=0)
        od = od + _dot(p.astype(BF16), expand) * pltpu.roll(gv, d, axis=0)

    qs = (gq * jnp.exp(b)).astype(BF16)
    ks = (gk * jnp.exp(bend - b)).astype(BF16)
    vs = gv.astype(BF16)
    dec = jnp.exp(bend)
    mr = lax.broadcasted_iota(jnp.int32, (GLA_VW, GLA_QK), 0) // GLA_DV
    mc = lax.broadcasted_iota(jnp.int32, (GLA_VW, GLA_QK), 1) // GLA_DK
    head_mask = (mr == mc).astype(F32)
    for j in range(nb):
        rs = slice(j * GLA_BLOCK, (j + 1) * GLA_BLOCK)
        ut_ref[j] = _dot_tn(vs[rs], ks[rs]) * head_mask
    st = st_ref[...]
    for j in range(nb):
        sb_ref[j] = st.astype(BF16)
        st = st * dec[j * GLA_BLOCK:j * GLA_BLOCK + 1, :] + ut_ref[j]
    st_ref[...] = st
    o_rows = [_dot_nt(qs[j * GLA_BLOCK:(j + 1) * GLA_BLOCK], sb_ref[j]) for j in range(nb)]

    gr = proj(C_GR, GLA_VW)
    o = jnp.concatenate(o_rows, axis=0) + od
    y_gla = _group_rms(o, ind64, GLA_DV) * glag_ref[...] * (gr * _sigmoid(gr))
    ycg_ref[:, CONV_CH:CONV_CH + GLA_VW] = y_gla.astype(BF16)

    tab = tab_ref[...]
    qtab = tab * ATTN_LOG2_SCALE
    qa = _dot(_rms(proj(C_CQ, MLA_Q_RANK), qg_ref[...]).astype(BF16), wuq_ref[...])
    kva = _dot(_rms(proj(C_CKV, MLA_KV_RANK), kvg_ref[...]).astype(BF16), wukv_ref[...])
    kr = proj(C_KR, 128) * tab
    lane = lax.broadcasted_iota(jnp.int32, (tm, 128), 1)
    kro = jnp.where(lane < MLA_ROPE, kr + pltpu.roll(kr, 64, axis=1), 0.0).astype(BF16)
    for hd in range(MLA_HEADS):
        c0 = hd * MLA_QK_PAD
        q_ref[:, c0:c0 + 128] = (qa[:, c0:c0 + 128] * ATTN_LOG2_SCALE).astype(BF16)
        qr = qa[:, c0 + 128:c0 + 256] * qtab
        q_ref[:, c0 + 128:c0 + 256] = (qr + pltpu.roll(qr, 64, axis=1)).astype(BF16)
        k_ref[:, c0:c0 + 128] = kva[:, c0:c0 + 128].astype(BF16)
        k_ref[:, c0 + 128:c0 + 256] = kro
        v_ref[:, hd * MLA_V:(hd + 1) * MLA_V] = kva[:, c0 + 128:c0 + 256].astype(BF16)


def _mixer_in(x, tab, lw, t, tm):
    row = lambda w: pl.BlockSpec((tm, w), lambda i: (i, 0))
    full = lambda a: pl.BlockSpec(a.shape, lambda i: (0,) * a.ndim)
    consts = [lw["attn_norm_g"], lw["w_in"], lw["conv_w"], lw["conv_norm_g"], lw["gla_w_gate"],
              lw["gla_b_gate"], lw["gla_norm_g"], lw["mla_q_norm_g"], lw["mla_w_uq"],
              lw["mla_kv_norm_g"], lw["mla_w_ukv"]]
    return pl.pallas_call(
        functools.partial(_mixer_kernel, tm=tm),
        grid=(t // tm,),
        in_specs=[row(D_MODEL), row(128)] + [full(a) for a in consts],
        out_specs=[row(CONV_CH + GLA_VW), row(MLA_HEADS * MLA_QK_PAD), row(MLA_HEADS * MLA_QK_PAD),
                   row(MLA_HEADS * MLA_V)],
        out_shape=[jax.ShapeDtypeStruct((t, CONV_CH + GLA_VW), BF16),
                   jax.ShapeDtypeStruct((t, MLA_HEADS * MLA_QK_PAD), BF16),
                   jax.ShapeDtypeStruct((t, MLA_HEADS * MLA_QK_PAD), BF16),
                   jax.ShapeDtypeStruct((t, MLA_HEADS * MLA_V), BF16)],
        scratch_shapes=[pltpu.VMEM((tm + 8, CONV_CH), F32),
                        pltpu.VMEM((GLA_VW, GLA_QK), F32),
                        pltpu.VMEM((tm // GLA_BLOCK, GLA_VW, GLA_QK), F32),
                        pltpu.VMEM((tm // GLA_BLOCK, GLA_VW, GLA_QK), BF16)],
        compiler_params=pltpu.CompilerParams(dimension_semantics=("arbitrary",),
                                             vmem_limit_bytes=V7X_VMEM_LIMIT_BYTES),
        name="mixer_in",
    )(x, tab, *consts)


def _attn_kernel(q_ref, k_ref, v_ref, g_ref, o_ref, s0, s1, p0, p1, al0, al1, m_sc, l_sc, acc_sc,
                 *, bq, bk):
    i = pl.program_id(1)
    nch = bk // 128
    s_bufs, p_bufs, al_bufs = (s0, s1), (p0, p1), (al0, al1)
    m_sc[...] = jnp.full(m_sc.shape, -jnp.inf, F32)
    l_sc[...] = jnp.zeros(l_sc.shape, F32)
    acc_sc[...] = jnp.zeros(acc_sc.shape, F32)

    def scores(j, slot, row0):
        r0 = pl.multiple_of(j * bk, bk)
        s_bufs[slot][row0:, :] = _dot_nt(q_ref[row0:, :], k_ref[pl.ds(r0, bk), :])

    def softmax(slot, diag):
        s_ref, p_ref, al_ref = s_bufs[slot], p_bufs[slot], al_bufs[slot]
        col0 = 0 if diag is None else diag * bk
        for r in range(col0 // ATTN_ROWS, bq // ATTN_ROWS):
            lo, hi = r * ATTN_ROWS, (r + 1) * ATTN_ROWS
            rs = slice(lo, hi)
            live = nch if diag is None else min(nch, (hi - 1 - col0) // 128 + 1)
            need_mask = [diag is not None and col0 + (c + 1) * 128 - 1 > lo for c in range(live)]
            if any(need_mask):
                rows = lax.broadcasted_iota(jnp.int32, (ATTN_ROWS, 128), 0) + (lo - col0)
                cols = lax.broadcasted_iota(jnp.int32, (ATTN_ROWS, 128), 1)

            def load(c):
                ch = s_ref[rs, c * 128:(c + 1) * 128]
                if need_mask[c]:
                    ch = jnp.where(cols + c * 128 <= rows, ch, -jnp.inf)
                return ch

            mx = functools.reduce(jnp.maximum, [load(c) for c in range(live)])
            m_prev = m_sc[rs, :]
            m_new = jnp.maximum(m_prev, jnp.max(mx, axis=-1, keepdims=True))
            alpha = jnp.exp2(m_prev - m_new)
            m_sc[rs, :] = m_new
            al_ref[rs, :] = alpha
            psum = None
            for c in range(nch):
                if c < live:
                    p = jnp.exp2(load(c) - m_new)
                    p_ref[rs, c * 128:(c + 1) * 128] = p.astype(BF16)
                    psum = p if psum is None else psum + p
                else:
                    p_ref[rs, c * 128:(c + 1) * 128] = jnp.zeros((ATTN_ROWS, 128), BF16)
            l_sc[rs, :] = alpha * l_sc[rs, :] + psum

    def pv(j, slot, row0):
        r0 = pl.multiple_of(j * bk, bk)
        acc_sc[row0:, :] = (al_bufs[slot][row0:, :] * acc_sc[row0:, :]
                            + _dot(p_bufs[slot][row0:, :], v_ref[pl.ds(r0, bk), :]))

    def pair(jj, carry):
        for a in range(2):
            scores(2 * jj + a, a, 0)
        for a in range(2):
            softmax(a, None)
            pv(2 * jj + a, a, 0)
        return carry

    lax.fori_loop(0, i, pair, 0)

    for c in range(2):
        scores(2 * i + c, c, c * bk)
    for c in range(2):
        softmax(c, c)
        pv(2 * i + c, c, c * bk)

    o = acc_sc[...] / jnp.sum(l_sc[...], axis=-1, keepdims=True)
    o_ref[...] = _rms(o, g_ref[...]).astype(BF16)


def _mla_attn(q, k, v, out_norm_g, t, bq, bk):
    assert bq == 2 * bk
    whole = functools.partial(pl.BlockSpec, pipeline_mode=pl.Buffered(1))
    return pl.pallas_call(
        functools.partial(_attn_kernel, bq=bq, bk=bk),
        grid=(MLA_HEADS, t // bq),
        in_specs=[pl.BlockSpec((bq, MLA_QK_PAD), lambda h, i: (i, h)),
                  whole((t, MLA_QK_PAD), lambda h, i: (0, h)),
                  whole((t, MLA_V), lambda h, i: (0, h)),
                  pl.BlockSpec((1, MLA_V), lambda h, i: (0, h))],
        out_specs=pl.BlockSpec((bq, MLA_V), lambda h, i: (i, h)),
        out_shape=jax.ShapeDtypeStruct((t, MLA_HEADS * MLA_V), BF16),
        scratch_shapes=[pltpu.VMEM((bq, bk), F32), pltpu.VMEM((bq, bk), F32),
                        pltpu.VMEM((bq, bk), BF16), pltpu.VMEM((bq, bk), BF16),
                        pltpu.VMEM((bq, 128), F32), pltpu.VMEM((bq, 128), F32),
                        pltpu.VMEM((bq, 128), F32), pltpu.VMEM((bq, 128), F32),
                        pltpu.VMEM((bq, MLA_V), F32)],
        compiler_params=pltpu.CompilerParams(dimension_semantics=("arbitrary", "arbitrary"),
                                             vmem_limit_bytes=V7X_VMEM_LIMIT_BYTES),
        name="mla_attn",
    )(q, k, v, out_norm_g)


def _out_ffn_kernel(x_ref, ycg_ref, ymla_ref, wout_ref, fg_ref, wgate_ref, wup_ref, wdown_ref,
                    og_ref, o_ref, *, final_norm):
    half = CONV_CH + GLA_VW
    x1 = (x_ref[...] + _dot(ycg_ref[...], wout_ref[0:half, :])
          + _dot(ymla_ref[...], wout_ref[half:2 * half, :]))
    h = _rms(x1, fg_ref[...]).astype(BF16)
    g = _dot(h, wgate_ref[...])
    u = _dot(h, wup_ref[...])
    a = (g * _sigmoid(g) * u).astype(BF16)
    out = x1 + _dot(a, wdown_ref[...])
    if final_norm:
        out = _rms(out, og_ref[...])
    o_ref[...] = out


def _out_ffn(x, ycg, ymla, lw, final_g, t, tm, final_norm):
    row = lambda w: pl.BlockSpec((tm, w), lambda i: (i, 0))
    resident = pl.BlockSpec(memory_space=pltpu.VMEM)
    return pl.pallas_call(
        functools.partial(_out_ffn_kernel, final_norm=final_norm),
        grid=(t // tm,),
        in_specs=[row(D_MODEL), row(CONV_CH + GLA_VW), row(MLA_HEADS * MLA_V),
                  resident, resident, resident, resident, resident, resident],
        out_specs=row(D_MODEL),
        out_shape=jax.ShapeDtypeStruct((t, D_MODEL), F32),
        compiler_params=pltpu.CompilerParams(dimension_semantics=("arbitrary",),
                                             vmem_limit_bytes=V7X_VMEM_LIMIT_BYTES),
        name="out_ffn",
    )(x, ycg, ymla, lw["w_out"], lw["ffn_norm_g"], lw["w_gate"], lw["w_up"], lw["w_down"], final_g)


def _pack_w_in(w):
    glow = jnp.pad(w[..., 1280:1296], ((0, 0), (0, 0), (0, 128 - GLA_GATE_RANK)))
    return jnp.concatenate(
        [w[..., 0:768], w[..., 768:1024], w[..., 1024:1280], w[..., 1296:1552], glow,
         w[..., 1552:1808], w[..., 1808:1936], w[..., 1936:2000], w[..., 1968:2000],
         w[..., 1936:1968]], axis=-1).astype(BF16)


def _pack_w_uq(w):
    depth = w.shape[0]
    w = w.reshape(depth, MLA_Q_RANK, MLA_HEADS, MLA_NOPE + MLA_ROPE)
    nope, r1, r2 = w[..., :MLA_NOPE], w[..., MLA_NOPE:MLA_NOPE + 32], w[..., MLA_NOPE + 32:]
    return jnp.concatenate([nope, r1, r2, r2, r1], axis=-1).reshape(
        depth, MLA_Q_RANK, MLA_HEADS * MLA_QK_PAD).astype(BF16)


def kernel(x, positions, attn_norm_g, w_in, conv_w, conv_norm_g, gla_w_gate, gla_b_gate, gla_norm_g, mla_q_norm_g, mla_w_uq, mla_kv_norm_g, mla_w_ukv, mla_out_norm_g, w_out, ffn_norm_g, w_gate, w_up, w_down, final_norm_g):
    batch, t, _ = x.shape
    assert batch == 1
    depth = w_in.shape[0]
    tm = min(t, 512)
    bq = min(t, 1024)
    bk = bq // 2
    assert t % tm == 0 and t % bq == 0 and tm % 128 == 0

    row = lambda a: a.reshape(depth, 1, a.shape[-1])
    packed = {
        "attn_norm_g": row(attn_norm_g),
        "w_in": _pack_w_in(w_in),
        "conv_w": jnp.pad(conv_w, ((0, 0), (0, 8 - conv_w.shape[1]), (0, 0))),
        "conv_norm_g": row(conv_norm_g),
        "gla_w_gate": jnp.pad(gla_w_gate, ((0, 0), (0, 128 - GLA_GATE_RANK), (0, 0))).astype(BF16),
        "gla_b_gate": row(gla_b_gate),
        "gla_norm_g": row(gla_norm_g),
        "mla_q_norm_g": row(mla_q_norm_g),
        "mla_w_uq": _pack_w_uq(mla_w_uq),
        "mla_kv_norm_g": row(mla_kv_norm_g),
        "mla_w_ukv": mla_w_ukv.astype(BF16),
        "mla_out_norm_g": row(mla_out_norm_g),
        "w_out": w_out.astype(BF16),
        "ffn_norm_g": row(ffn_norm_g),
        "w_gate": w_gate.astype(BF16),
        "w_up": w_up.astype(BF16),
        "w_down": w_down.astype(BF16),
    }
    final_g = final_norm_g.reshape(1, D_MODEL)

    tab = _rope_table(positions, t)
    xs = x.reshape(t, D_MODEL)
    for l in range(depth):
        lw = {name: a[l] for name, a in packed.items()}
        ycg, q, k, v = _mixer_in(xs, tab, lw, t, tm)
        ymla = _mla_attn(q, k, v, lw["mla_out_norm_g"], t, bq, bk)
        xs = _out_ffn(xs, ycg, ymla, lw, final_g, t, tm, final_norm=(l == depth - 1))
    return xs.reshape(batch, t, D_MODEL)
```

```python
import functools

import jax
import jax.numpy as jnp
from jax import lax
from jax.experimental import pallas as pl
from jax.experimental.pallas import tpu as pltpu

D_MODEL = 1024
CONV_CH = 256
CONV_GROUP_W = 64
GLA_HEADS = 4
GLA_DK = 32
GLA_DV = 64
GLA_GATE_RANK = 16
GLA_TAU = 16.0
GLA_QK = GLA_HEADS * GLA_DK
GLA_VW = GLA_HEADS * GLA_DV
GLA_BLOCK = 16
MLA_HEADS = 4
MLA_Q_RANK = 256
MLA_KV_RANK = 128
MLA_NOPE = 128
MLA_ROPE = 64
MLA_V = 128
MLA_QK_PAD = 256
ATTN_ROWS = 64
ROPE_THETA = 10000.0
D_FF = 2816
EPS = 1e-6
ATTN_LOG2_SCALE = (MLA_NOPE + MLA_ROPE) ** -0.5 * 1.4426950408889634

C_CONV = 0
C_QK = 768
C_GV = 1024
C_GR = 1280
C_GLOW = 1536
C_CQ = 1664
C_CKV = 1920
C_KR = 2048
D_IN_PACKED = 2176

V7X_VMEM_LIMIT_BYTES = 58 * 1024 * 1024

F32 = jnp.float32
BF16 = jnp.bfloat16
NEG_BIG = -1e30


def _dot(a, b):
    return jnp.dot(a, b, preferred_element_type=F32)


def _dot_nt(a, b):
    return lax.dot_general(a, b, (((1,), (1,)), ((), ())), preferred_element_type=F32)


def _dot_tn(a, b):
    return lax.dot_general(a, b, (((0,), (0,)), ((), ())), preferred_element_type=F32)


def _split3(x):
    hi = x.astype(BF16)
    r = x - hi.astype(F32)
    mid = r.astype(BF16)
    lo = (r - mid.astype(F32)).astype(BF16)
    return hi, mid, lo


def _rms(x, g):
    return x * lax.rsqrt(jnp.mean(x * x, axis=-1, keepdims=True) + EPS) * g


def _group_rms(x, ind, width):
    sq = x * x
    hi = sq.astype(BF16)
    lo = (sq - hi.astype(F32)).astype(BF16)
    ms = (_dot(hi, ind) + _dot(lo, ind)) * (1.0 / width)
    return x * lax.rsqrt(ms + EPS)


def _sigmoid(x):
    return 1.0 / (1.0 + jnp.exp(-x))


def _rope_kernel(pos_ref, inv_ref, tab_ref):
    ang = pos_ref[...].astype(F32) * inv_ref[...]
    lane = lax.broadcasted_iota(jnp.int32, ang.shape, 1)
    c = jnp.cos(ang)
    s = jnp.sin(ang)
    tab_ref[...] = jnp.where(lane < 64, c, jnp.where(lane < 96, -s, s))


def _rope_table(positions, t):
    tr = min(t, 1024)
    inv = 1.0 / (ROPE_THETA ** (jnp.arange(0, MLA_ROPE, 2, dtype=F32) / MLA_ROPE))
    inv4 = jnp.tile(inv, 4).reshape(1, 128)
    return pl.pallas_call(
        _rope_kernel,
        grid=(t // tr,),
        in_specs=[pl.BlockSpec((tr, 1), lambda i: (i, 0)),
                  pl.BlockSpec((1, 128), lambda i: (0, 0))],
        out_specs=pl.BlockSpec((tr, 128), lambda i: (i, 0)),
        out_shape=jax.ShapeDtypeStruct((t, 128), F32),
        name="rope_table",
    )(positions.reshape(t, 1), inv4)


def _mixer_kernel(x_ref, tab_ref, ang_ref, win_ref, convw_ref, convg_ref, wg_ref, bg_ref, glag_ref,
                  qg_ref, wuq_ref, kvg_ref, wukv_ref, ind_ref, tribl_ref, expand_ref, hmask_ref,
                  ycg_ref, q_ref, kt_ref, v_ref,
                  vbuf, st_ref, ut_ref, sb_ref, *, tm):
    nb = tm // GLA_BLOCK

    @pl.when(pl.program_id(0) == 0)
    def _():
        vbuf[0:8, :] = jnp.zeros((8, CONV_CH), F32)
        st_ref[...] = jnp.zeros(st_ref.shape, F32)

    h = _rms(x_ref[...], ang_ref[...]).astype(BF16)

    def proj(c0, n):
        return _dot(h, win_ref[:, c0:c0 + n])

    ind64 = ind_ref[...]

    zc = proj(C_CONV, 3 * CONV_CH)
    cb = zc[:, 0:CONV_CH]
    v = zc[:, CONV_CH:2 * CONV_CH] * zc[:, 2 * CONV_CH:3 * CONV_CH]
    vbuf[8:8 + tm, :] = v
    y = (convw_ref[2:3, :] * v + convw_ref[1:2, :] * vbuf[7:7 + tm, :]
         + convw_ref[0:1, :] * vbuf[6:6 + tm, :])
    vbuf[0:8, :] = vbuf[tm:tm + 8, :]
    y_conv = _group_rms(cb * y, ind64, CONV_GROUP_W) * convg_ref[...]
    ycg_ref[:, 0:CONV_CH] = y_conv.astype(BF16)

    zqk = proj(C_QK, 2 * GLA_QK)
    gq = zqk[:, 0:GLA_QK] * (GLA_DK ** -0.5)
    gk = zqk[:, GLA_QK:2 * GLA_QK]
    gv = proj(C_GV, GLA_VW)
    glow = proj(C_GLOW, 128)
    xg = _dot(glow.astype(BF16), wg_ref[...]) + bg_ref[...]
    la = (jnp.minimum(xg, 0.0) - jnp.log(1.0 + jnp.exp(-jnp.abs(xg)))) * (1.0 / GLA_TAU)

    cat = jnp.concatenate(_split3(la), axis=1)
    tri_blk = tribl_ref[...]
    b_rows, e_rows = [], []
    for g in range(tm // 128):
        r3 = _dot(tri_blk, cat[g * 128:(g + 1) * 128])
        r = r3[:, 0:128] + r3[:, 128:256] + r3[:, 256:384]
        b_rows.append(r[0:128])
        e_rows.append(r[128:256])
    b = jnp.concatenate(b_rows, axis=0)
    bend = jnp.concatenate(e_rows, axis=0)

    expand = expand_ref[...]
    tpos = lax.broadcasted_iota(jnp.int32, (tm, GLA_QK), 0) % GLA_BLOCK
    od = _dot((gq * gk).astype(BF16), expand) * gv
    for d in range(1, GLA_BLOCK):
        arg = jnp.where(tpos >= d, b - pltpu.roll(b, d, axis=0), NEG_BIG)
        p = gq * jnp.exp(arg) * pltpu.roll(gk, d, axis=0)
        od = od + _dot(p.astype(BF16), expand) * pltpu.roll(gv, d, axis=0)

    qs = (gq * jnp.exp(b)).astype(BF16)
    ks = (gk * jnp.exp(bend - b)).astype(BF16)
    vs = gv.astype(BF16)
    dec = jnp.exp(bend)
    head_mask = hmask_ref[...]
    for j in range(nb):
        rs = slice(j * GLA_BLOCK, (j + 1) * GLA_BLOCK)
        ut_ref[j] = _dot_tn(vs[rs], ks[rs]) * head_mask
    st = st_ref[...]
    for j in range(nb):
        sb_ref[j] = st.astype(BF16)
        st = st * dec[j * GLA_BLOCK:j * GLA_BLOCK + 1, :] + ut_ref[j]
    st_ref[...] = st
    o_rows = [_dot_nt(qs[j * GLA_BLOCK:(j + 1) * GLA_BLOCK], sb_ref[j]) for j in range(nb)]

    gr = proj(C_GR, GLA_VW)
    o = jnp.concatenate(o_rows, axis=0) + od
    y_gla = _group_rms(o, ind64, GLA_DV) * glag_ref[...] * (gr * _sigmoid(gr))
    ycg_ref[:, CONV_CH:CONV_CH + GLA_VW] = y_gla.astype(BF16)

    tab = tab_ref[...]
    qtab = tab * ATTN_LOG2_SCALE
    qa = _dot(_rms(proj(C_CQ, MLA_Q_RANK), qg_ref[...]).astype(BF16), wuq_ref[...])
    kva = _dot(_rms(proj(C_CKV, MLA_KV_RANK), kvg_ref[...]).astype(BF16), wukv_ref[...])
    kr = proj(C_KR, 128) * tab
    lane = lax.broadcasted_iota(jnp.int32, (tm, 128), 1)
    kro = jnp.where(lane < MLA_ROPE, kr + pltpu.roll(kr, 64, axis=1), 0.0)
    kro_t = kro.T.astype(BF16)
    for hd in range(MLA_HEADS):
        c0 = hd * MLA_QK_PAD
        q_ref[:, c0:c0 + 128] = (qa[:, c0:c0 + 128] * ATTN_LOG2_SCALE).astype(BF16)
        qr = qa[:, c0 + 128:c0 + 256] * qtab
        q_ref[:, c0 + 128:c0 + 256] = (qr + pltpu.roll(qr, 64, axis=1)).astype(BF16)
        kt_ref[hd, 0, 0:128, :] = kva[:, c0:c0 + 128].T.astype(BF16)
        kt_ref[hd, 0, 128:256, :] = kro_t
        v_ref[:, hd * MLA_V:(hd + 1) * MLA_V] = kva[:, c0 + 128:c0 + 256].astype(BF16)


def _mixer_constants():
    idx = lambda n: jnp.arange(n, dtype=jnp.int32)
    ind64 = (idx(256)[:, None] // 64 == idx(256)[None, :] // 64).astype(BF16)
    same = idx(128)[:, None] // GLA_BLOCK == idx(128)[None, :] // GLA_BLOCK
    tri = same & (idx(128)[None, :] <= idx(128)[:, None])
    tri_blk = jnp.concatenate([tri, same], axis=0).astype(BF16)
    expand = (idx(GLA_QK)[:, None] // GLA_DK == idx(GLA_VW)[None, :] // GLA_DV).astype(BF16)
    head_mask = (idx(GLA_VW)[:, None] // GLA_DV == idx(GLA_QK)[None, :] // GLA_DK).astype(F32)
    return ind64, tri_blk, expand, head_mask


def _mixer_in(x, tab, lw, t, tm):
    row = lambda w: pl.BlockSpec((tm, w), lambda i: (i, 0))
    full = lambda a: pl.BlockSpec(a.shape, lambda i: (0,) * a.ndim)
    consts = [lw["attn_norm_g"], lw["w_in"], lw["conv_w"], lw["conv_norm_g"], lw["gla_w_gate"],
              lw["gla_b_gate"], lw["gla_norm_g"], lw["mla_q_norm_g"], lw["mla_w_uq"],
              lw["mla_kv_norm_g"], lw["mla_w_ukv"]] + list(_mixer_constants())
    return pl.pallas_call(
        functools.partial(_mixer_kernel, tm=tm),
        grid=(t // tm,),
        in_specs=[row(D_MODEL), row(128)] + [full(a) for a in consts],
        out_specs=[row(CONV_CH + GLA_VW), row(MLA_HEADS * MLA_QK_PAD),
                   pl.BlockSpec((MLA_HEADS, 1, MLA_QK_PAD, tm), lambda i: (0, i, 0, 0)),
                   row(MLA_HEADS * MLA_V)],
        out_shape=[jax.ShapeDtypeStruct((t, CONV_CH + GLA_VW), BF16),
                   jax.ShapeDtypeStruct((t, MLA_HEADS * MLA_QK_PAD), BF16),
                   jax.ShapeDtypeStruct((MLA_HEADS, t // tm, MLA_QK_PAD, tm), BF16),
                   jax.ShapeDtypeStruct((t, MLA_HEADS * MLA_V), BF16)],
        scratch_shapes=[pltpu.VMEM((tm + 8, CONV_CH), F32),
                        pltpu.VMEM((GLA_VW, GLA_QK), F32),
                        pltpu.VMEM((tm // GLA_BLOCK, GLA_VW, GLA_QK), F32),
                        pltpu.VMEM((tm // GLA_BLOCK, GLA_VW, GLA_QK), BF16)],
        compiler_params=pltpu.CompilerParams(dimension_semantics=("arbitrary",),
                                             vmem_limit_bytes=V7X_VMEM_LIMIT_BYTES),
        name="mixer_in",
    )(x, tab, *consts)


def _attn_kernel(q_ref, kt_ref, v_ref, g_ref, o_ref, s0, s1, p0, p1, al0, al1, m_sc, l_sc, acc_sc,
                 *, bq, bk):
    i = pl.program_id(1)
    nch = bk // 128
    s_bufs, p_bufs, al_bufs = (s0, s1), (p0, p1), (al0, al1)
    m_sc[...] = jnp.full(m_sc.shape, -jnp.inf, F32)
    l_sc[...] = jnp.zeros(l_sc.shape, F32)
    acc_sc[...] = jnp.zeros(acc_sc.shape, F32)

    def scores(j, slot, row0):
        s_bufs[slot][row0:, 0:bk] = _dot(q_ref[row0:, :], kt_ref[j])

    def softmax(slot, diag):
        s_ref, p_ref, al_ref = s_bufs[slot], p_bufs[slot], al_bufs[slot]
        col0 = 0 if diag is None else diag * bk
        for r in range(col0 // ATTN_ROWS, bq // ATTN_ROWS):
            lo, hi = r * ATTN_ROWS, (r + 1) * ATTN_ROWS
            rs = slice(lo, hi)
            live = nch if diag is None else min(nch, (hi - 1 - col0) // 128 + 1)
            need_mask = [diag is not None and col0 + (c + 1) * 128 - 1 > lo for c in range(live)]
            if any(need_mask):
                rows = lax.broadcasted_iota(jnp.int32, (ATTN_ROWS, 128), 0) + (lo - col0)
                cols = lax.broadcasted_iota(jnp.int32, (ATTN_ROWS, 128), 1)

            def load(c):
                ch = s_ref[rs, c * 128:(c + 1) * 128]
                if need_mask[c]:
                    ch = jnp.where(cols + c * 128 <= rows, ch, -jnp.inf)
                return ch

            mx = functools.reduce(jnp.maximum, [load(c) for c in range(live)])
            m_prev = m_sc[rs, :]
            m_new = jnp.maximum(m_prev, jnp.max(mx, axis=-1, keepdims=True))
            alpha = jnp.exp2(m_prev - m_new)
            m_sc[rs, :] = m_new
            al_ref[rs, :] = alpha
            psum = None
            for c in range(nch):
                if c < live:
                    p = jnp.exp2(load(c) - m_new)
                    p_ref[rs, c * 128:(c + 1) * 128] = p.astype(BF16)
                    psum = p if psum is None else psum + p
                else:
                    p_ref[rs, c * 128:(c + 1) * 128] = jnp.zeros((ATTN_ROWS, 128), BF16)
            l_sc[rs, :] = alpha * l_sc[rs, :] + psum

    def pv(j, slot, row0):
        r0 = pl.multiple_of(j * bk, bk)
        acc_sc[row0:, :] = (al_bufs[slot][row0:, :] * acc_sc[row0:, :]
                            + _dot(p_bufs[slot][row0:, 0:bk], v_ref[pl.ds(r0, bk), :]))

    def pair(jj, carry):
        for a in range(2):
            scores(2 * jj + a, a, 0)
        for a in range(2):
            softmax(a, None)
            pv(2 * jj + a, a, 0)
        return carry

    lax.fori_loop(0, i, pair, 0)

    for c in range(2):
        scores(2 * i + c, c, c * bk)
    for c in range(2):
        softmax(c, c)
        pv(2 * i + c, c, c * bk)

    o = acc_sc[...] / jnp.sum(l_sc[...], axis=-1, keepdims=True)
    o_ref[...] = _rms(o, g_ref[...]).astype(BF16)


def _mla_attn(q, kt, v, out_norm_g, t, bq, bk):
    assert bq == 2 * bk
    whole = functools.partial(pl.BlockSpec, pipeline_mode=pl.Buffered(1))
    return pl.pallas_call(
        functools.partial(_attn_kernel, bq=bq, bk=bk),
        grid=(MLA_HEADS, t // bq),
        in_specs=[pl.BlockSpec((bq, MLA_QK_PAD), lambda h, i: (i, h)),
                  whole((None, t // bk, MLA_QK_PAD, bk), lambda h, i: (h, 0, 0, 0)),
                  whole((t, MLA_V), lambda h, i: (0, h)),
                  pl.BlockSpec((1, MLA_V), lambda h, i: (0, h))],
        out_specs=pl.BlockSpec((bq, MLA_V), lambda h, i: (i, h)),
        out_shape=jax.ShapeDtypeStruct((t, MLA_HEADS * MLA_V), BF16),
        scratch_shapes=[pltpu.VMEM((bq, bk + 128), F32), pltpu.VMEM((bq, bk + 128), F32),
                        pltpu.VMEM((bq, bk + 128), BF16), pltpu.VMEM((bq, bk + 128), BF16),
                        pltpu.VMEM((bq, 128), F32), pltpu.VMEM((bq, 128), F32),
                        pltpu.VMEM((bq, 128), F32), pltpu.VMEM((bq, 128), F32),
                        pltpu.VMEM((bq, MLA_V), F32)],
        compiler_params=pltpu.CompilerParams(dimension_semantics=("arbitrary", "arbitrary"),
                                             vmem_limit_bytes=V7X_VMEM_LIMIT_BYTES),
        name="mla_attn",
    )(q, kt, v, out_norm_g)


def _out_ffn_kernel(x_ref, ycg_ref, ymla_ref, wout_ref, fg_ref, wgate_ref, wup_ref, wdown_ref,
                    og_ref, o_ref, *, final_norm):
    half = CONV_CH + GLA_VW
    x1 = (x_ref[...] + _dot(ycg_ref[...], wout_ref[0:half, :])
          + _dot(ymla_ref[...], wout_ref[half:2 * half, :]))
    h = _rms(x1, fg_ref[...]).astype(BF16)
    g = _dot(h, wgate_ref[...])
    u = _dot(h, wup_ref[...])
    a = (g * _sigmoid(g) * u).astype(BF16)
    out = x1 + _dot(a, wdown_ref[...])
    if final_norm:
        out = _rms(out, og_ref[...])
    o_ref[...] = out


def _out_ffn(x, ycg, ymla, lw, final_g, t, tm, final_norm):
    row = lambda w: pl.BlockSpec((tm, w), lambda i: (i, 0))
    resident = pl.BlockSpec(memory_space=pltpu.VMEM)
    return pl.pallas_call(
        functools.partial(_out_ffn_kernel, final_norm=final_norm),
        grid=(t // tm,),
        in_specs=[row(D_MODEL), row(CONV_CH + GLA_VW), row(MLA_HEADS * MLA_V),
                  resident, resident, resident, resident, resident, resident],
        out_specs=row(D_MODEL),
        out_shape=jax.ShapeDtypeStruct((t, D_MODEL), F32),
        compiler_params=pltpu.CompilerParams(dimension_semantics=("arbitrary",),
                                             vmem_limit_bytes=V7X_VMEM_LIMIT_BYTES),
        name="out_ffn",
    )(x, ycg, ymla, lw["w_out"], lw["ffn_norm_g"], lw["w_gate"], lw["w_up"], lw["w_down"], final_g)


def _pack_w_in(w):
    w = w.astype(BF16)
    glow = jnp.pad(w[..., 1280:1296], ((0, 0), (0, 0), (0, 128 - GLA_GATE_RANK)))
    return jnp.concatenate(
        [w[..., 0:768], w[..., 768:1024], w[..., 1024:1280], w[..., 1296:1552], glow,
         w[..., 1552:1808], w[..., 1808:1936], w[..., 1936:2000], w[..., 1968:2000],
         w[..., 1936:1968]], axis=-1)


def _pack_w_uq(w):
    depth = w.shape[0]
    w = w.astype(BF16).reshape(depth, MLA_Q_RANK, MLA_HEADS, MLA_NOPE + MLA_ROPE)
    nope, r1, r2 = w[..., :MLA_NOPE], w[..., MLA_NOPE:MLA_NOPE + 32], w[..., MLA_NOPE + 32:]
    return jnp.concatenate([nope, r1, r2, r2, r1], axis=-1).reshape(
        depth, MLA_Q_RANK, MLA_HEADS * MLA_QK_PAD)


def kernel(x, positions, attn_norm_g, w_in, conv_w, conv_norm_g, gla_w_gate, gla_b_gate, gla_norm_g, mla_q_norm_g, mla_w_uq, mla_kv_norm_g, mla_w_ukv, mla_out_norm_g, w_out, ffn_norm_g, w_gate, w_up, w_down, final_norm_g):
    batch, t, _ = x.shape
    assert batch == 1
    depth = w_in.shape[0]
    tm = min(t, 512)
    bq = min(t, 1024)
    bk = bq // 2
    assert t % tm == 0 and t % bq == 0 and tm % 128 == 0 and bk == tm

    row = lambda a: a.reshape(depth, 1, a.shape[-1])
    packed = {
        "attn_norm_g": row(attn_norm_g),
        "w_in": _pack_w_in(w_in),
        "conv_w": jnp.pad(conv_w, ((0, 0), (0, 8 - conv_w.shape[1]), (0, 0))),
        "conv_norm_g": row(conv_norm_g),
        "gla_w_gate": jnp.pad(gla_w_gate, ((0, 0), (0, 128 - GLA_GATE_RANK), (0, 0))).astype(BF16),
        "gla_b_gate": row(gla_b_gate),
        "gla_norm_g": row(gla_norm_g),
        "mla_q_norm_g": row(mla_q_norm_g),
        "mla_w_uq": _pack_w_uq(mla_w_uq),
        "mla_kv_norm_g": row(mla_kv_norm_g),
        "mla_w_ukv": mla_w_ukv.astype(BF16),
        "mla_out_norm_g": row(mla_out_norm_g),
        "w_out": w_out.astype(BF16),
        "ffn_norm_g": row(ffn_norm_g),
        "w_gate": w_gate.astype(BF16),
        "w_up": w_up.astype(BF16),
        "w_down": w_down.astype(BF16),
    }
    final_g = final_norm_g.reshape(1, D_MODEL)

    tab = _rope_table(positions, t)
    xs = x.reshape(t, D_MODEL)
    for l in range(depth):
        lw = {name: a[l] for name, a in packed.items()}
        ycg, q, kt, v = _mixer_in(xs, tab, lw, t, tm)
        ymla = _mla_attn(q, kt, v, lw["mla_out_norm_g"], t, bq, bk)
        xs = _out_ffn(xs, ycg, ymla, lw, final_g, t, tm, final_norm=(l == depth - 1))
    return xs.reshape(batch, t, D_MODEL)
```

```python
import functools

import jax
import jax.numpy as jnp
from jax import lax
from jax.experimental import pallas as pl
from jax.experimental.pallas import tpu as pltpu

D_MODEL = 1024
CONV_CH = 256
CONV_GROUP_W = 64
GLA_HEADS = 4
GLA_DK = 32
GLA_DV = 64
GLA_GATE_RANK = 16
GLA_TAU = 16.0
GLA_QK = GLA_HEADS * GLA_DK
GLA_VW = GLA_HEADS * GLA_DV
GLA_BLOCK = 16
MLA_HEADS = 4
MLA_Q_RANK = 256
MLA_KV_RANK = 128
MLA_NOPE = 128
MLA_ROPE = 64
MLA_V = 128
MLA_QK_PAD = 256
ATTN_ROWS = 64
ROPE_THETA = 10000.0
D_FF = 2816
EPS = 1e-6
ATTN_LOG2_SCALE = (MLA_NOPE + MLA_ROPE) ** -0.5 * 1.4426950408889634

C_CONV = 0
C_QK = 768
C_GV = 1024
D_IN_HEAD = 1280
T_GR = 0
T_GLOW = 256
T_CQ = 384
T_CKV = 640
T_KR = 768
D_IN_TAIL = 896

V7X_VMEM_LIMIT_BYTES = 58 * 1024 * 1024

F32 = jnp.float32
BF16 = jnp.bfloat16
NEG_BIG = -1e30


def _dot(a, b):
    return jnp.dot(a, b, preferred_element_type=F32)


def _dot_nt(a, b):
    return lax.dot_general(a, b, (((1,), (1,)), ((), ())), preferred_element_type=F32)


def _dot_tn(a, b):
    return lax.dot_general(a, b, (((0,), (0,)), ((), ())), preferred_element_type=F32)


def _split3(x):
    hi = x.astype(BF16)
    r = x - hi.astype(F32)
    mid = r.astype(BF16)
    lo = (r - mid.astype(F32)).astype(BF16)
    return hi, mid, lo


def _rms(x, g):
    return x * lax.rsqrt(jnp.mean(x * x, axis=-1, keepdims=True) + EPS) * g


def _group_rms(x, ind, width):
    sq = x * x
    hi = sq.astype(BF16)
    lo = (sq - hi.astype(F32)).astype(BF16)
    ms = (_dot(hi, ind) + _dot(lo, ind)) * (1.0 / width)
    return x * lax.rsqrt(ms + EPS)


def _sigmoid(x):
    return 1.0 / (1.0 + jnp.exp(-x))


def _rope_kernel(pos_ref, inv_ref, tab_ref):
    ang = pos_ref[...].astype(F32) * inv_ref[...]
    lane = lax.broadcasted_iota(jnp.int32, ang.shape, 1)
    c = jnp.cos(ang)
    s = jnp.sin(ang)
    tab_ref[...] = jnp.where(lane < 64, c, jnp.where(lane < 96, -s, s))


def _rope_table(positions, t):
    tr = min(t, 1024)
    inv = 1.0 / (ROPE_THETA ** (jnp.arange(0, MLA_ROPE, 2, dtype=F32) / MLA_ROPE))
    inv4 = jnp.tile(inv, 4).reshape(1, 128)
    return pl.pallas_call(
        _rope_kernel,
        grid=(t // tr,),
        in_specs=[pl.BlockSpec((tr, 1), lambda i: (i, 0)),
                  pl.BlockSpec((1, 128), lambda i: (0, 0))],
        out_specs=pl.BlockSpec((tr, 128), lambda i: (i, 0)),
        out_shape=jax.ShapeDtypeStruct((t, 128), F32),
        name="rope_table",
    )(positions.reshape(t, 1), inv4)


def _mixer_kernel(x_ref, tab_ref, ang_ref, win_ref, wtail_ref, convw_ref, convg_ref, wg_ref, bg_ref, glag_ref,
                  qg_ref, wuq_ref, kvg_ref, wukv_ref, ind_ref, tribl_ref, expand_ref, hmask_ref,
                  ycg_ref, q_ref, kt_ref, v_ref,
                  vbuf, st_ref, ut_ref, sb_ref, *, tm):
    nb = tm // GLA_BLOCK

    @pl.when(pl.program_id(0) == 0)
    def _():
        vbuf[0:8, :] = jnp.zeros((8, CONV_CH), F32)
        st_ref[...] = jnp.zeros(st_ref.shape, F32)

    h = _rms(x_ref[...], ang_ref[...]).astype(BF16)

    def proj(c0, n):
        return _dot(h, win_ref[:, c0:c0 + n])

    def proj_tail(c0, n):
        return _dot(h, wtail_ref[:, c0:c0 + n])

    ind64 = ind_ref[...]

    zc = proj(C_CONV, 3 * CONV_CH)
    cb = zc[:, 0:CONV_CH]
    v = zc[:, CONV_CH:2 * CONV_CH] * zc[:, 2 * CONV_CH:3 * CONV_CH]
    vbuf[8:8 + tm, :] = v
    y = (convw_ref[2:3, :] * v + convw_ref[1:2, :] * vbuf[7:7 + tm, :]
         + convw_ref[0:1, :] * vbuf[6:6 + tm, :])
    vbuf[0:8, :] = vbuf[tm:tm + 8, :]
    y_conv = _group_rms(cb * y, ind64, CONV_GROUP_W) * convg_ref[...]
    ycg_ref[:, 0:CONV_CH] = y_conv.astype(BF16)

    zqk = proj(C_QK, 2 * GLA_QK)
    gq = zqk[:, 0:GLA_QK] * (GLA_DK ** -0.5)
    gk = zqk[:, GLA_QK:2 * GLA_QK]
    gv = proj(C_GV, GLA_VW)
    glow = proj_tail(T_GLOW, 128)
    xg = _dot(glow.astype(BF16), wg_ref[...]) + bg_ref[...]
    la = (jnp.minimum(xg, 0.0) - jnp.log(1.0 + jnp.exp(-jnp.abs(xg)))) * (1.0 / GLA_TAU)

    cat = jnp.concatenate(_split3(la), axis=1)
    tri_blk = tribl_ref[...]
    b_rows, e_rows = [], []
    for g in range(tm // 128):
        r3 = _dot(tri_blk, cat[g * 128:(g + 1) * 128])
        r = r3[:, 0:128] + r3[:, 128:256] + r3[:, 256:384]
        b_rows.append(r[0:128])
        e_rows.append(r[128:256])
    b = jnp.concatenate(b_rows, axis=0)
    bend = jnp.concatenate(e_rows, axis=0)

    expand = expand_ref[...]
    tpos = lax.broadcasted_iota(jnp.int32, (tm, GLA_QK), 0) % GLA_BLOCK
    od = _dot((gq * gk).astype(BF16), expand) * gv
    for d in range(1, GLA_BLOCK):
        arg = jnp.where(tpos >= d, b - pltpu.roll(b, d, axis=0), NEG_BIG)
        p = gq * jnp.exp(arg) * pltpu.roll(gk, d, axis=0)
        od = od + _dot(p.astype(BF16), expand) * pltpu.roll(gv, d, axis=0)

    qs = (gq * jnp.exp(b)).astype(BF16)
    ks = (gk * jnp.exp(bend - b)).astype(BF16)
    vs = gv.astype(BF16)
    dec = jnp.exp(bend)
    head_mask = hmask_ref[...]
    for j in range(nb):
        rs = slice(j * GLA_BLOCK, (j + 1) * GLA_BLOCK)
        ut_ref[j] = _dot_tn(vs[rs], ks[rs]) * head_mask
    st = st_ref[...]
    for j in range(nb):
        sb_ref[j] = st.astype(BF16)
        st = st * dec[j * GLA_BLOCK:j * GLA_BLOCK + 1, :] + ut_ref[j]
    st_ref[...] = st
    o_rows = [_dot_nt(qs[j * GLA_BLOCK:(j + 1) * GLA_BLOCK], sb_ref[j]) for j in range(nb)]

    gr = proj_tail(T_GR, GLA_VW)
    o = jnp.concatenate(o_rows, axis=0) + od
    y_gla = _group_rms(o, ind64, GLA_DV) * glag_ref[...] * (gr * _sigmoid(gr))
    ycg_ref[:, CONV_CH:CONV_CH + GLA_VW] = y_gla.astype(BF16)

    tab = tab_ref[...]
    qtab = tab * ATTN_LOG2_SCALE
    qa = _dot(_rms(proj_tail(T_CQ, MLA_Q_RANK), qg_ref[...]).astype(BF16), wuq_ref[...])
    kva = _dot(_rms(proj_tail(T_CKV, MLA_KV_RANK), kvg_ref[...]).astype(BF16), wukv_ref[...])
    kr = proj_tail(T_KR, 128) * tab
    lane = lax.broadcasted_iota(jnp.int32, (tm, 128), 1)
    kro = jnp.where(lane < MLA_ROPE, kr + pltpu.roll(kr, 64, axis=1), 0.0)
    kro_t = kro.T.astype(BF16)
    for hd in range(MLA_HEADS):
        c0 = hd * MLA_QK_PAD
        q_ref[:, c0:c0 + 128] = (qa[:, c0:c0 + 128] * ATTN_LOG2_SCALE).astype(BF16)
        qr = qa[:, c0 + 128:c0 + 256] * qtab
        q_ref[:, c0 + 128:c0 + 256] = (qr + pltpu.roll(qr, 64, axis=1)).astype(BF16)
        kt_ref[hd, 0, 0:128, :] = kva[:, c0:c0 + 128].T.astype(BF16)
        kt_ref[hd, 0, 128:256, :] = kro_t
        v_ref[:, hd * MLA_V:(hd + 1) * MLA_V] = kva[:, c0 + 128:c0 + 256].astype(BF16)


def _mixer_constants():
    idx = lambda n: jnp.arange(n, dtype=jnp.int32)
    ind64 = (idx(256)[:, None] // 64 == idx(256)[None, :] // 64).astype(BF16)
    same = idx(128)[:, None] // GLA_BLOCK == idx(128)[None, :] // GLA_BLOCK
    tri = same & (idx(128)[None, :] <= idx(128)[:, None])
    tri_blk = jnp.concatenate([tri, same], axis=0).astype(BF16)
    expand = (idx(GLA_QK)[:, None] // GLA_DK == idx(GLA_VW)[None, :] // GLA_DV).astype(BF16)
    head_mask = (idx(GLA_VW)[:, None] // GLA_DV == idx(GLA_QK)[None, :] // GLA_DK).astype(F32)
    return ind64, tri_blk, expand, head_mask


def _mixer_in(x, tab, pw, layer, t, tm):
    row = lambda w: pl.BlockSpec((tm, w), lambda i: (i, 0))
    full = lambda a: pl.BlockSpec(a.shape, lambda i: (0,) * a.ndim)
    of_layer = lambda a, cols=None: pl.BlockSpec((None, a.shape[1], cols or a.shape[2]),
                                                 lambda i: (layer, 0, 0))
    stacked = [pw["attn_norm_g"], pw["w_in"], pw["w_in_tail"], pw["conv_w"], pw["conv_norm_g"],
               pw["gla_w_gate"], pw["gla_b_gate"], pw["gla_norm_g"], pw["mla_q_norm_g"], pw["mla_w_uq"],
               pw["mla_kv_norm_g"], pw["mla_w_ukv"]]
    stacked_specs = [of_layer(a, D_IN_HEAD if a is pw["w_in"] else None) for a in stacked]
    consts = list(_mixer_constants())
    return pl.pallas_call(
        functools.partial(_mixer_kernel, tm=tm),
        grid=(t // tm,),
        in_specs=[row(D_MODEL), row(128)] + stacked_specs + [full(a) for a in consts],
        out_specs=[row(CONV_CH + GLA_VW), row(MLA_HEADS * MLA_QK_PAD),
                   pl.BlockSpec((MLA_HEADS, 1, MLA_QK_PAD, tm), lambda i: (0, i, 0, 0)),
                   row(MLA_HEADS * MLA_V)],
        out_shape=[jax.ShapeDtypeStruct((t, CONV_CH + GLA_VW), BF16),
                   jax.ShapeDtypeStruct((t, MLA_HEADS * MLA_QK_PAD), BF16),
                   jax.ShapeDtypeStruct((MLA_HEADS, t // tm, MLA_QK_PAD, tm), BF16),
                   jax.ShapeDtypeStruct((t, MLA_HEADS * MLA_V), BF16)],
        scratch_shapes=[pltpu.VMEM((tm + 8, CONV_CH), F32),
                        pltpu.VMEM((GLA_VW, GLA_QK), F32),
                        pltpu.VMEM((tm // GLA_BLOCK, GLA_VW, GLA_QK), F32),
                        pltpu.VMEM((tm // GLA_BLOCK, GLA_VW, GLA_QK), BF16)],
        compiler_params=pltpu.CompilerParams(dimension_semantics=("arbitrary",),
                                             vmem_limit_bytes=V7X_VMEM_LIMIT_BYTES),
        name="mixer_in",
    )(x, tab, *stacked, *consts)


def _attn_kernel(q_ref, kt_ref, v_ref, g_ref, o_ref, s0, s1, p0, p1, al0, al1, m_sc, l_sc, acc_sc,
                 *, bq, bk):
    i = pl.program_id(1)
    nch = bk // 128
    s_bufs, p_bufs, al_bufs = (s0, s1), (p0, p1), (al0, al1)
    m_sc[...] = jnp.full(m_sc.shape, -jnp.inf, F32)
    l_sc[...] = jnp.zeros(l_sc.shape, F32)
    acc_sc[...] = jnp.zeros(acc_sc.shape, F32)

    def scores(j, slot, row0):
        s_bufs[slot][row0:, 0:bk] = _dot(q_ref[row0:, :], kt_ref[j])

    def softmax(slot, diag):
        s_ref, p_ref, al_ref = s_bufs[slot], p_bufs[slot], al_bufs[slot]
        col0 = 0 if diag is None else diag * bk
        for r in range(col0 // ATTN_ROWS, bq // ATTN_ROWS):
            lo, hi = r * ATTN_ROWS, (r + 1) * ATTN_ROWS
            rs = slice(lo, hi)
            live = nch if diag is None else min(nch, (hi - 1 - col0) // 128 + 1)
            need_mask = [diag is not None and col0 + (c + 1) * 128 - 1 > lo for c in range(live)]
            if any(need_mask):
                rows = lax.broadcasted_iota(jnp.int32, (ATTN_ROWS, 128), 0) + (lo - col0)
                cols = lax.broadcasted_iota(jnp.int32, (ATTN_ROWS, 128), 1)

            def load(c):
                ch = s_ref[rs, c * 128:(c + 1) * 128]
                if need_mask[c]:
                    ch = jnp.where(cols + c * 128 <= rows, ch, -jnp.inf)
                return ch

            mx = functools.reduce(jnp.maximum, [load(c) for c in range(live)])
            m_prev = m_sc[rs, :]
            m_new = jnp.maximum(m_prev, jnp.max(mx, axis=-1, keepdims=True))
            alpha = jnp.exp2(m_prev - m_new)
            m_sc[rs, :] = m_new
            al_ref[rs, :] = alpha
            psum = None
            for c in range(nch):
                if c < live:
                    p = jnp.exp2(load(c) - m_new)
                    p_ref[rs, c * 128:(c + 1) * 128] = p.astype(BF16)
                    psum = p if psum is None else psum + p
                else:
                    p_ref[rs, c * 128:(c + 1) * 128] = jnp.zeros((ATTN_ROWS, 128), BF16)
            l_sc[rs, :] = alpha * l_sc[rs, :] + psum

    def pv(j, slot, row0):
        r0 = pl.multiple_of(j * bk, bk)
        acc_sc[row0:, :] = (al_bufs[slot][row0:, :] * acc_sc[row0:, :]
                            + _dot(p_bufs[slot][row0:, 0:bk], v_ref[pl.ds(r0, bk), :]))

    def pair(jj, carry):
        for a in range(2):
            scores(2 * jj + a, a, 0)
        for a in range(2):
            softmax(a, None)
            pv(2 * jj + a, a, 0)
        return carry

    lax.fori_loop(0, i, pair, 0)

    for c in range(2):
        scores(2 * i + c, c, c * bk)
    for c in range(2):
        softmax(c, c)
        pv(2 * i + c, c, c * bk)

    o = acc_sc[...] / jnp.sum(l_sc[...], axis=-1, keepdims=True)
    o_ref[...] = _rms(o, g_ref[...]).astype(BF16)


def _mla_attn(q, kt, v, out_norm_g, layer, t, bq, bk):
    assert bq == 2 * bk
    whole = functools.partial(pl.BlockSpec, pipeline_mode=pl.Buffered(1))
    return pl.pallas_call(
        functools.partial(_attn_kernel, bq=bq, bk=bk),
        grid=(MLA_HEADS, t // bq),
        in_specs=[pl.BlockSpec((bq, MLA_QK_PAD), lambda h, i: (i, h)),
                  whole((None, t // bk, MLA_QK_PAD, bk), lambda h, i: (h, 0, 0, 0)),
                  whole((t, MLA_V), lambda h, i: (0, h)),
                  pl.BlockSpec((None, 1, MLA_V), lambda h, i: (layer, 0, h))],
        out_specs=pl.BlockSpec((bq, MLA_V), lambda h, i: (i, h)),
        out_shape=jax.ShapeDtypeStruct((t, MLA_HEADS * MLA_V), BF16),
        scratch_shapes=[pltpu.VMEM((bq, bk + 128), F32), pltpu.VMEM((bq, bk + 128), F32),
                        pltpu.VMEM((bq, bk + 128), BF16), pltpu.VMEM((bq, bk + 128), BF16),
                        pltpu.VMEM((bq, 128), F32), pltpu.VMEM((bq, 128), F32),
                        pltpu.VMEM((bq, 128), F32), pltpu.VMEM((bq, 128), F32),
                        pltpu.VMEM((bq, MLA_V), F32)],
        compiler_params=pltpu.CompilerParams(dimension_semantics=("arbitrary", "arbitrary"),
                                             vmem_limit_bytes=V7X_VMEM_LIMIT_BYTES),
        name="mla_attn",
    )(q, kt, v, out_norm_g)


def _out_ffn_kernel(x_ref, ycg_ref, ymla_ref, wout_ref, fg_ref, wgate_ref, wup_ref, wdown_ref,
                    og_ref, o_ref, *, final_norm):
    half = CONV_CH + GLA_VW
    x1 = (x_ref[...] + _dot(ycg_ref[...], wout_ref[0:half, :])
          + _dot(ymla_ref[...], wout_ref[half:2 * half, :]))
    h = _rms(x1, fg_ref[...]).astype(BF16)
    g = _dot(h, wgate_ref[...])
    u = _dot(h, wup_ref[...])
    a = (g * _sigmoid(g) * u).astype(BF16)
    out = x1 + _dot(a, wdown_ref[...])
    if final_norm:
        out = _rms(out, og_ref[...])
    o_ref[...] = out


def _out_ffn(x, ycg, ymla, pw, layer, final_g, t, tm, final_norm):
    row = lambda w: pl.BlockSpec((tm, w), lambda i: (i, 0))
    resident = lambda a: pl.BlockSpec((None,) + a.shape[1:], lambda i: (layer, 0, 0),
                                      pipeline_mode=pl.Buffered(1))
    weights = [pw["w_out"], pw["ffn_norm_g"], pw["w_gate"], pw["w_up"], pw["w_down"]]
    return pl.pallas_call(
        functools.partial(_out_ffn_kernel, final_norm=final_norm),
        grid=(t // tm,),
        in_specs=[row(D_MODEL), row(CONV_CH + GLA_VW), row(MLA_HEADS * MLA_V)]
                 + [resident(a) for a in weights] + [pl.BlockSpec(final_g.shape, lambda i: (0, 0))],
        out_specs=row(D_MODEL),
        out_shape=jax.ShapeDtypeStruct((t, D_MODEL), F32),
        compiler_params=pltpu.CompilerParams(dimension_semantics=("arbitrary",),
                                             vmem_limit_bytes=V7X_VMEM_LIMIT_BYTES),
        name="out_ffn",
    )(x, ycg, ymla, *weights, final_g)


def _pack_w_in_tail(w):
    glow = jnp.pad(w[..., 1280:1296], ((0, 0), (0, 0), (0, 128 - GLA_GATE_RANK)))
    return jnp.concatenate(
        [w[..., 1296:1552], glow, w[..., 1552:1808], w[..., 1808:1936], w[..., 1936:2000],
         w[..., 1968:2000], w[..., 1936:1968]], axis=-1)


def _pack_w_uq(w):
    depth = w.shape[0]
    w = w.astype(BF16).reshape(depth, MLA_Q_RANK, MLA_HEADS, MLA_NOPE + MLA_ROPE)
    nope, r1, r2 = w[..., :MLA_NOPE], w[..., MLA_NOPE:MLA_NOPE + 32], w[..., MLA_NOPE + 32:]
    return jnp.concatenate([nope, r1, r2, r2, r1], axis=-1).reshape(
        depth, MLA_Q_RANK, MLA_HEADS * MLA_QK_PAD)


def kernel(x, positions, attn_norm_g, w_in, conv_w, conv_norm_g, gla_w_gate, gla_b_gate, gla_norm_g, mla_q_norm_g, mla_w_uq, mla_kv_norm_g, mla_w_ukv, mla_out_norm_g, w_out, ffn_norm_g, w_gate, w_up, w_down, final_norm_g):
    batch, t, _ = x.shape
    assert batch == 1
    depth = w_in.shape[0]
    tm = min(t, 512)
    bq = min(t, 1024)
    bk = bq // 2
    assert t % tm == 0 and t % bq == 0 and tm % 128 == 0 and bk == tm

    row = lambda a: a.reshape(depth, 1, a.shape[-1])
    w_in_bf16 = w_in.astype(BF16)
    packed = {
        "attn_norm_g": row(attn_norm_g),
        "w_in": w_in_bf16,
        "w_in_tail": _pack_w_in_tail(w_in_bf16),
        "conv_w": jnp.pad(conv_w, ((0, 0), (0, 8 - conv_w.shape[1]), (0, 0))),
        "conv_norm_g": row(conv_norm_g),
        "gla_w_gate": jnp.pad(gla_w_gate, ((0, 0), (0, 128 - GLA_GATE_RANK), (0, 0))).astype(BF16),
        "gla_b_gate": row(gla_b_gate),
        "gla_norm_g": row(gla_norm_g),
        "mla_q_norm_g": row(mla_q_norm_g),
        "mla_w_uq": _pack_w_uq(mla_w_uq),
        "mla_kv_norm_g": row(mla_kv_norm_g),
        "mla_w_ukv": mla_w_ukv.astype(BF16),
        "mla_out_norm_g": row(mla_out_norm_g),
        "w_out": w_out.astype(BF16),
        "ffn_norm_g": row(ffn_norm_g),
        "w_gate": w_gate.astype(BF16),
        "w_up": w_up.astype(BF16),
        "w_down": w_down.astype(BF16),
    }
    final_g = final_norm_g.reshape(1, D_MODEL)

    tab = _rope_table(positions, t)
    xs = x.reshape(t, D_MODEL)
    for l in range(depth):
        ycg, q, kt, v = _mixer_in(xs, tab, packed, l, t, tm)
        ymla = _mla_attn(q, kt, v, packed["mla_out_norm_g"], l, t, bq, bk)
        xs = _out_ffn(xs, ycg, ymla, packed, l, final_g, t, tm, final_norm=(l == depth - 1))
    return xs.reshape(batch, t, D_MODEL)
```

```python
import functools

import jax
import jax.numpy as jnp
from jax import lax
from jax.experimental import pallas as pl
from jax.experimental.pallas import tpu as pltpu

D_MODEL = 1024
CONV_CH = 256
CONV_GROUP_W = 64
GLA_HEADS = 4
GLA_DK = 32
GLA_DV = 64
GLA_GATE_RANK = 16
GLA_TAU = 16.0
GLA_QK = GLA_HEADS * GLA_DK
GLA_VW = GLA_HEADS * GLA_DV
GLA_BLOCK = 16
MLA_HEADS = 4
MLA_Q_RANK = 256
MLA_KV_RANK = 128
MLA_NOPE = 128
MLA_ROPE = 64
MLA_V = 128
MLA_QK_PAD = 256
KT_BLOCK = 512
ATTN_ROWS = 64
ROPE_THETA = 10000.0
D_FF = 2816
EPS = 1e-6
ATTN_LOG2_SCALE = (MLA_NOPE + MLA_ROPE) ** -0.5 * 1.4426950408889634

C_CONV = 0
C_QK = 768
C_GV = 1024
D_IN_HEAD = 1280
T_GR = 0
T_GLOW = 256
T_CQ = 384
T_CKV = 640
T_KR = 768
D_IN_TAIL = 896

V7X_VMEM_LIMIT_BYTES = 58 * 1024 * 1024

F32 = jnp.float32
BF16 = jnp.bfloat16
NEG_BIG = -1e30


def _dot(a, b):
    return jnp.dot(a, b, preferred_element_type=F32)


def _dot_nt(a, b):
    return lax.dot_general(a, b, (((1,), (1,)), ((), ())), preferred_element_type=F32)


def _dot_tn(a, b):
    return lax.dot_general(a, b, (((0,), (0,)), ((), ())), preferred_element_type=F32)


def _split3(x):
    hi = x.astype(BF16)
    r = x - hi.astype(F32)
    mid = r.astype(BF16)
    lo = (r - mid.astype(F32)).astype(BF16)
    return hi, mid, lo


def _rms(x, g):
    return x * lax.rsqrt(jnp.mean(x * x, axis=-1, keepdims=True) + EPS) * g


def _group_rms(x, ind, width):
    sq = x * x
    hi = sq.astype(BF16)
    lo = (sq - hi.astype(F32)).astype(BF16)
    ms = (_dot(hi, ind) + _dot(lo, ind)) * (1.0 / width)
    return x * lax.rsqrt(ms + EPS)


def _sigmoid(x):
    return 1.0 / (1.0 + jnp.exp(-x))


def _rope_kernel(pos_ref, inv_ref, tab_ref):
    ang = pos_ref[...].astype(F32) * inv_ref[...]
    lane = lax.broadcasted_iota(jnp.int32, ang.shape, 1)
    c = jnp.cos(ang)
    s = jnp.sin(ang)
    tab_ref[...] = jnp.where(lane < 64, c, jnp.where(lane < 96, -s, s))


def _rope_table(positions, t):
    tr = min(t, 1024)
    inv = 1.0 / (ROPE_THETA ** (jnp.arange(0, MLA_ROPE, 2, dtype=F32) / MLA_ROPE))
    inv4 = jnp.tile(inv, 4).reshape(1, 128)
    return pl.pallas_call(
        _rope_kernel,
        grid=(t // tr,),
        in_specs=[pl.BlockSpec((tr, 1), lambda i: (i, 0)),
                  pl.BlockSpec((1, 128), lambda i: (0, 0))],
        out_specs=pl.BlockSpec((tr, 128), lambda i: (i, 0)),
        out_shape=jax.ShapeDtypeStruct((t, 128), F32),
        name="rope_table",
    )(positions.reshape(t, 1), inv4)


def _mixer_kernel(x_ref, tab_ref, ang_ref, win_ref, wtail_ref, convw_ref, convg_ref, wg_ref, bg_ref, glag_ref,
                  qg_ref, wuq_ref, kvg_ref, wukv_ref, ind_ref, tribl_ref, expand_ref, hmask_ref,
                  ycg_ref, q_ref, kt_ref, v_ref,
                  vbuf, st_ref, ut_ref, sb_ref, *, tm):
    nb = tm // GLA_BLOCK

    @pl.when(pl.program_id(0) == 0)
    def _():
        vbuf[0:8, :] = jnp.zeros((8, CONV_CH), F32)
        st_ref[...] = jnp.zeros(st_ref.shape, F32)

    h = _rms(x_ref[...], ang_ref[...]).astype(BF16)

    def proj(c0, n):
        return _dot(h, win_ref[:, c0:c0 + n])

    def proj_tail(c0, n):
        return _dot(h, wtail_ref[:, c0:c0 + n])

    ind64 = ind_ref[...]

    zc = proj(C_CONV, 3 * CONV_CH)
    cb = zc[:, 0:CONV_CH]
    v = zc[:, CONV_CH:2 * CONV_CH] * zc[:, 2 * CONV_CH:3 * CONV_CH]
    vbuf[8:8 + tm, :] = v
    y = (convw_ref[2:3, :] * v + convw_ref[1:2, :] * vbuf[7:7 + tm, :]
         + convw_ref[0:1, :] * vbuf[6:6 + tm, :])
    vbuf[0:8, :] = vbuf[tm:tm + 8, :]
    y_conv = _group_rms(cb * y, ind64, CONV_GROUP_W) * convg_ref[...]
    ycg_ref[:, 0:CONV_CH] = y_conv.astype(BF16)

    zqk = proj(C_QK, 2 * GLA_QK)
    gq = zqk[:, 0:GLA_QK] * (GLA_DK ** -0.5)
    gk = zqk[:, GLA_QK:2 * GLA_QK]
    gv = proj(C_GV, GLA_VW)
    glow = proj_tail(T_GLOW, 128)
    xg = _dot(glow.astype(BF16), wg_ref[...]) + bg_ref[...]
    la = (jnp.minimum(xg, 0.0) - jnp.log(1.0 + jnp.exp(-jnp.abs(xg)))) * (1.0 / GLA_TAU)

    cat = jnp.concatenate(_split3(la), axis=1)
    tri_blk = tribl_ref[...]
    b_rows, e_rows = [], []
    for g in range(tm // 128):
        r3 = _dot(tri_blk, cat[g * 128:(g + 1) * 128])
        r = r3[:, 0:128] + r3[:, 128:256] + r3[:, 256:384]
        b_rows.append(r[0:128])
        e_rows.append(r[128:256])
    b = jnp.concatenate(b_rows, axis=0)
    bend = jnp.concatenate(e_rows, axis=0)

    expand = expand_ref[...]
    tpos = lax.broadcasted_iota(jnp.int32, (tm, GLA_QK), 0) % GLA_BLOCK
    od = _dot((gq * gk).astype(BF16), expand) * gv
    for d in range(1, GLA_BLOCK):
        arg = jnp.where(tpos >= d, b - pltpu.roll(b, d, axis=0), NEG_BIG)
        p = gq * jnp.exp(arg) * pltpu.roll(gk, d, axis=0)
        od = od + _dot(p.astype(BF16), expand) * pltpu.roll(gv, d, axis=0)

    qs = (gq * jnp.exp(b)).astype(BF16)
    ks = (gk * jnp.exp(bend - b)).astype(BF16)
    vs = gv.astype(BF16)
    dec = jnp.exp(bend)
    head_mask = hmask_ref[...]
    for j in range(nb):
        rs = slice(j * GLA_BLOCK, (j + 1) * GLA_BLOCK)
        ut_ref[j] = _dot_tn(vs[rs], ks[rs]) * head_mask
    st = st_ref[...]
    for j in range(nb):
        sb_ref[j] = st.astype(BF16)
        st = st * dec[j * GLA_BLOCK:j * GLA_BLOCK + 1, :] + ut_ref[j]
    st_ref[...] = st
    o_rows = [_dot_nt(qs[j * GLA_BLOCK:(j + 1) * GLA_BLOCK], sb_ref[j]) for j in range(nb)]

    gr = proj_tail(T_GR, GLA_VW)
    o = jnp.concatenate(o_rows, axis=0) + od
    y_gla = _group_rms(o, ind64, GLA_DV) * glag_ref[...] * (gr * _sigmoid(gr))
    ycg_ref[:, CONV_CH:CONV_CH + GLA_VW] = y_gla.astype(BF16)

    tab = tab_ref[...]
    qtab = tab * ATTN_LOG2_SCALE
    qa = _dot(_rms(proj_tail(T_CQ, MLA_Q_RANK), qg_ref[...]).astype(BF16), wuq_ref[...])
    kva = _dot(_rms(proj_tail(T_CKV, MLA_KV_RANK), kvg_ref[...]).astype(BF16), wukv_ref[...])
    kr = proj_tail(T_KR, 128) * tab
    lane = lax.broadcasted_iota(jnp.int32, (tm, 128), 1)
    kro = jnp.where(lane < MLA_ROPE, kr + pltpu.roll(kr, 64, axis=1), 0.0)
    kro_t = kro.T.astype(BF16)
    nkt = tm // KT_BLOCK
    for hd in range(MLA_HEADS):
        c0 = hd * MLA_QK_PAD
        q_ref[:, c0:c0 + 128] = (qa[:, c0:c0 + 128] * ATTN_LOG2_SCALE).astype(BF16)
        qr = qa[:, c0 + 128:c0 + 256] * qtab
        q_ref[:, c0 + 128:c0 + 256] = (qr + pltpu.roll(qr, 64, axis=1)).astype(BF16)
        knt = kva[:, c0:c0 + 128].T.astype(BF16)
        for part in range(nkt):
            cs = slice(part * KT_BLOCK, (part + 1) * KT_BLOCK)
            kt_ref[hd, part, 0:128, :] = knt[:, cs]
            kt_ref[hd, part, 128:256, :] = kro_t[:, cs]
        v_ref[:, hd * MLA_V:(hd + 1) * MLA_V] = kva[:, c0 + 128:c0 + 256].astype(BF16)


def _mixer_constants():
    idx = lambda n: jnp.arange(n, dtype=jnp.int32)
    ind64 = (idx(256)[:, None] // 64 == idx(256)[None, :] // 64).astype(BF16)
    same = idx(128)[:, None] // GLA_BLOCK == idx(128)[None, :] // GLA_BLOCK
    tri = same & (idx(128)[None, :] <= idx(128)[:, None])
    tri_blk = jnp.concatenate([tri, same], axis=0).astype(BF16)
    expand = (idx(GLA_QK)[:, None] // GLA_DK == idx(GLA_VW)[None, :] // GLA_DV).astype(BF16)
    head_mask = (idx(GLA_VW)[:, None] // GLA_DV == idx(GLA_QK)[None, :] // GLA_DK).astype(F32)
    return ind64, tri_blk, expand, head_mask


def _mixer_in(x, tab, pw, layer, t, tm):
    row = lambda w: pl.BlockSpec((tm, w), lambda i: (i, 0))
    full = lambda a: pl.BlockSpec(a.shape, lambda i: (0,) * a.ndim)
    of_layer = lambda a, cols=None: pl.BlockSpec((None, a.shape[1], cols or a.shape[2]),
                                                 lambda i: (layer, 0, 0))
    stacked = [pw["attn_norm_g"], pw["w_in"], pw["w_in_tail"], pw["conv_w"], pw["conv_norm_g"],
               pw["gla_w_gate"], pw["gla_b_gate"], pw["gla_norm_g"], pw["mla_q_norm_g"], pw["mla_w_uq"],
               pw["mla_kv_norm_g"], pw["mla_w_ukv"]]
    stacked_specs = [of_layer(a, D_IN_HEAD if a is pw["w_in"] else None) for a in stacked]
    consts = list(_mixer_constants())
    return pl.pallas_call(
        functools.partial(_mixer_kernel, tm=tm),
        grid=(t // tm,),
        in_specs=[row(D_MODEL), row(128)] + stacked_specs + [full(a) for a in consts],
        out_specs=[row(CONV_CH + GLA_VW), row(MLA_HEADS * MLA_QK_PAD),
                   pl.BlockSpec((MLA_HEADS, tm // KT_BLOCK, MLA_QK_PAD, KT_BLOCK), lambda i: (0, i, 0, 0)),
                   row(MLA_HEADS * MLA_V)],
        out_shape=[jax.ShapeDtypeStruct((t, CONV_CH + GLA_VW), BF16),
                   jax.ShapeDtypeStruct((t, MLA_HEADS * MLA_QK_PAD), BF16),
                   jax.ShapeDtypeStruct((MLA_HEADS, t // KT_BLOCK, MLA_QK_PAD, KT_BLOCK), BF16),
                   jax.ShapeDtypeStruct((t, MLA_HEADS * MLA_V), BF16)],
        scratch_shapes=[pltpu.VMEM((tm + 8, CONV_CH), F32),
                        pltpu.VMEM((GLA_VW, GLA_QK), F32),
                        pltpu.VMEM((tm // GLA_BLOCK, GLA_VW, GLA_QK), F32),
                        pltpu.VMEM((tm // GLA_BLOCK, GLA_VW, GLA_QK), BF16)],
        compiler_params=pltpu.CompilerParams(dimension_semantics=("arbitrary",),
                                             vmem_limit_bytes=V7X_VMEM_LIMIT_BYTES),
        name="mixer_in",
    )(x, tab, *stacked, *consts)


def _attn_kernel(q_ref, kt_ref, v_ref, g_ref, o_ref, s0, s1, p0, p1, al0, al1, m_sc, l_sc, acc_sc,
                 *, bq, bk):
    i = pl.program_id(1)
    nch = bk // 128
    s_bufs, p_bufs, al_bufs = (s0, s1), (p0, p1), (al0, al1)
    m_sc[...] = jnp.full(m_sc.shape, -jnp.inf, F32)
    l_sc[...] = jnp.zeros(l_sc.shape, F32)
    acc_sc[...] = jnp.zeros(acc_sc.shape, F32)

    def scores(j, slot, row0):
        s_bufs[slot][row0:, 0:bk] = _dot(q_ref[row0:, :], kt_ref[j])

    def softmax(slot, diag):
        s_ref, p_ref, al_ref = s_bufs[slot], p_bufs[slot], al_bufs[slot]
        col0 = 0 if diag is None else diag * bk
        for r in range(col0 // ATTN_ROWS, bq // ATTN_ROWS):
            lo, hi = r * ATTN_ROWS, (r + 1) * ATTN_ROWS
            rs = slice(lo, hi)
            live = nch if diag is None else min(nch, (hi - 1 - col0) // 128 + 1)
            need_mask = [diag is not None and col0 + (c + 1) * 128 - 1 > lo for c in range(live)]
            if any(need_mask):
                rows = lax.broadcasted_iota(jnp.int32, (ATTN_ROWS, 128), 0) + (lo - col0)
                cols = lax.broadcasted_iota(jnp.int32, (ATTN_ROWS, 128), 1)

            def load(c):
                ch = s_ref[rs, c * 128:(c + 1) * 128]
                if need_mask[c]:
                    ch = jnp.where(cols + c * 128 <= rows, ch, -jnp.inf)
                return ch

            mx = functools.reduce(jnp.maximum, [load(c) for c in range(live)])
            m_prev = m_sc[rs, :]
            m_new = jnp.maximum(m_prev, jnp.max(mx, axis=-1, keepdims=True))
            alpha = jnp.exp2(m_prev - m_new)
            m_sc[rs, :] = m_new
            al_ref[rs, :] = alpha
            psum = None
            for c in range(nch):
                if c < live:
                    p = jnp.exp2(load(c) - m_new)
                    p_ref[rs, c * 128:(c + 1) * 128] = p.astype(BF16)
                    psum = p if psum is None else psum + p
                else:
                    p_ref[rs, c * 128:(c + 1) * 128] = jnp.zeros((ATTN_ROWS, 128), BF16)
            l_sc[rs, :] = alpha * l_sc[rs, :] + psum

    def pv(j, slot, row0):
        r0 = pl.multiple_of(j * bk, bk)
        acc_sc[row0:, :] = (al_bufs[slot][row0:, :] * acc_sc[row0:, :]
                            + _dot(p_bufs[slot][row0:, 0:bk], v_ref[pl.ds(r0, bk), :]))

    def pair(jj, carry):
        for a in range(2):
            scores(2 * jj + a, a, 0)
        for a in range(2):
            softmax(a, None)
            pv(2 * jj + a, a, 0)
        return carry

    lax.fori_loop(0, i, pair, 0)

    for c in range(2):
        scores(2 * i + c, c, c * bk)
    for c in range(2):
        softmax(c, c)
        pv(2 * i + c, c, c * bk)

    o = acc_sc[...] / jnp.sum(l_sc[...], axis=-1, keepdims=True)
    o_ref[...] = _rms(o, g_ref[...]).astype(BF16)


def _mla_attn(q, kt, v, out_norm_g, layer, t, bq, bk):
    assert bq == 2 * bk
    whole = functools.partial(pl.BlockSpec, pipeline_mode=pl.Buffered(1))
    return pl.pallas_call(
        functools.partial(_attn_kernel, bq=bq, bk=bk),
        grid=(MLA_HEADS, t // bq),
        in_specs=[pl.BlockSpec((bq, MLA_QK_PAD), lambda h, i: (i, h)),
                  whole((None, t // bk, MLA_QK_PAD, bk), lambda h, i: (h, 0, 0, 0)),
                  whole((t, MLA_V), lambda h, i: (0, h)),
                  pl.BlockSpec((None, 1, MLA_V), lambda h, i: (layer, 0, h))],
        out_specs=pl.BlockSpec((bq, MLA_V), lambda h, i: (i, h)),
        out_shape=jax.ShapeDtypeStruct((t, MLA_HEADS * MLA_V), BF16),
        scratch_shapes=[pltpu.VMEM((bq, bk + 128), F32), pltpu.VMEM((bq, bk + 128), F32),
                        pltpu.VMEM((bq, bk + 128), BF16), pltpu.VMEM((bq, bk + 128), BF16),
                        pltpu.VMEM((bq, 128), F32), pltpu.VMEM((bq, 128), F32),
                        pltpu.VMEM((bq, 128), F32), pltpu.VMEM((bq, 128), F32),
                        pltpu.VMEM((bq, MLA_V), F32)],
        compiler_params=pltpu.CompilerParams(dimension_semantics=("arbitrary", "arbitrary"),
                                             vmem_limit_bytes=V7X_VMEM_LIMIT_BYTES),
        name="mla_attn",
    )(q, kt, v, out_norm_g)


def _out_ffn_kernel(x_ref, ycg_ref, ymla_ref, wout_ref, fg_ref, wgate_ref, wup_ref, wdown_ref,
                    og_ref, o_ref, *, final_norm):
    half = CONV_CH + GLA_VW
    x1 = (x_ref[...] + _dot(ycg_ref[...], wout_ref[0:half, :])
          + _dot(ymla_ref[...], wout_ref[half:2 * half, :]))
    h = _rms(x1, fg_ref[...]).astype(BF16)
    g = _dot(h, wgate_ref[...])
    u = _dot(h, wup_ref[...])
    a = (g * _sigmoid(g) * u).astype(BF16)
    out = x1 + _dot(a, wdown_ref[...])
    if final_norm:
        out = _rms(out, og_ref[...])
    o_ref[...] = out


def _out_ffn(x, ycg, ymla, pw, layer, final_g, t, tm, final_norm):
    row = lambda w: pl.BlockSpec((tm, w), lambda i: (i, 0))
    resident = lambda a: pl.BlockSpec((None,) + a.shape[1:], lambda i: (layer, 0, 0),
                                      pipeline_mode=pl.Buffered(1))
    weights = [pw["w_out"], pw["ffn_norm_g"], pw["w_gate"], pw["w_up"], pw["w_down"]]
    return pl.pallas_call(
        functools.partial(_out_ffn_kernel, final_norm=final_norm),
        grid=(t // tm,),
        in_specs=[row(D_MODEL), row(CONV_CH + GLA_VW), row(MLA_HEADS * MLA_V)]
                 + [resident(a) for a in weights] + [pl.BlockSpec(final_g.shape, lambda i: (0, 0))],
        out_specs=row(D_MODEL),
        out_shape=jax.ShapeDtypeStruct((t, D_MODEL), F32),
        compiler_params=pltpu.CompilerParams(dimension_semantics=("arbitrary",),
                                             vmem_limit_bytes=V7X_VMEM_LIMIT_BYTES),
        name="out_ffn",
    )(x, ycg, ymla, *weights, final_g)


def _pack_w_in_tail(w):
    glow = jnp.pad(w[..., 1280:1296], ((0, 0), (0, 0), (0, 128 - GLA_GATE_RANK)))
    return jnp.concatenate(
        [w[..., 1296:1552], glow, w[..., 1552:1808], w[..., 1808:1936], w[..., 1936:2000],
         w[..., 1968:2000], w[..., 1936:1968]], axis=-1)


def _pack_w_uq(w):
    depth = w.shape[0]
    w = w.astype(BF16).reshape(depth, MLA_Q_RANK, MLA_HEADS, MLA_NOPE + MLA_ROPE)
    nope, r1, r2 = w[..., :MLA_NOPE], w[..., MLA_NOPE:MLA_NOPE + 32], w[..., MLA_NOPE + 32:]
    return jnp.concatenate([nope, r1, r2, r2, r1], axis=-1).reshape(
        depth, MLA_Q_RANK, MLA_HEADS * MLA_QK_PAD)


def kernel(x, positions, attn_norm_g, w_in, conv_w, conv_norm_g, gla_w_gate, gla_b_gate, gla_norm_g, mla_q_norm_g, mla_w_uq, mla_kv_norm_g, mla_w_ukv, mla_out_norm_g, w_out, ffn_norm_g, w_gate, w_up, w_down, final_norm_g):
    batch, t, _ = x.shape
    assert batch == 1
    depth = w_in.shape[0]
    tm = min(t, 1024)
    tm_ffn = min(t, 512)
    bq = min(t, 1024)
    bk = bq // 2
    assert t % tm == 0 and t % bq == 0 and tm % 128 == 0 and bk == KT_BLOCK

    row = lambda a: a.reshape(depth, 1, a.shape[-1])
    w_in_bf16 = w_in.astype(BF16)
    packed = {
        "attn_norm_g": row(attn_norm_g),
        "w_in": w_in_bf16,
        "w_in_tail": _pack_w_in_tail(w_in_bf16),
        "conv_w": jnp.pad(conv_w, ((0, 0), (0, 8 - conv_w.shape[1]), (0, 0))),
        "conv_norm_g": row(conv_norm_g),
        "gla_w_gate": jnp.pad(gla_w_gate, ((0, 0), (0, 128 - GLA_GATE_RANK), (0, 0))).astype(BF16),
        "gla_b_gate": row(gla_b_gate),
        "gla_norm_g": row(gla_norm_g),
        "mla_q_norm_g": row(mla_q_norm_g),
        "mla_w_uq": _pack_w_uq(mla_w_uq),
        "mla_kv_norm_g": row(mla_kv_norm_g),
        "mla_w_ukv": mla_w_ukv.astype(BF16),
        "mla_out_norm_g": row(mla_out_norm_g),
        "w_out": w_out.astype(BF16),
        "ffn_norm_g": row(ffn_norm_g),
        "w_gate": w_gate.astype(BF16),
        "w_up": w_up.astype(BF16),
        "w_down": w_down.astype(BF16),
    }
    final_g = final_norm_g.reshape(1, D_MODEL)

    tab = _rope_table(positions, t)
    xs = x.reshape(t, D_MODEL)
    for l in range(depth):
        ycg, q, kt, v = _mixer_in(xs, tab, packed, l, t, tm)
        ymla = _mla_attn(q, kt, v, packed["mla_out_norm_g"], l, t, bq, bk)
        xs = _out_ffn(xs, ycg, ymla, packed, l, final_g, t, tm_ffn, final_norm=(l == depth - 1))
    return xs.reshape(batch, t, D_MODEL)
```

```python
import functools

import jax
import jax.numpy as jnp
from jax import lax
from jax.experimental import pallas as pl
from jax.experimental.pallas import tpu as pltpu

D_MODEL = 1024
CONV_CH = 256
CONV_GROUP_W = 64
GLA_HEADS = 4
GLA_DK = 32
GLA_DV = 64
GLA_GATE_RANK = 16
GLA_TAU = 16.0
GLA_QK = GLA_HEADS * GLA_DK
GLA_VW = GLA_HEADS * GLA_DV
GLA_BLOCK = 16
MLA_HEADS = 4
MLA_Q_RANK = 256
MLA_KV_RANK = 128
MLA_NOPE = 128
MLA_ROPE = 64
MLA_V = 128
MLA_QK_PAD = 256
KT_BLOCK = 512
ATTN_ROWS = 64
ROPE_THETA = 10000.0
D_FF = 2816
FF_CHUNK = 256
EPS = 1e-6
ATTN_LOG2_SCALE = (MLA_NOPE + MLA_ROPE) ** -0.5 * 1.4426950408889634

C_CONV = 0
C_QK = 768
C_GV = 1024
D_IN_HEAD = 1280
T_GR = 0
T_GLOW = 256
T_CQ = 384
T_CKV = 640
T_KR = 768
D_IN_TAIL = 896

V7X_VMEM_LIMIT_BYTES = 58 * 1024 * 1024

F32 = jnp.float32
BF16 = jnp.bfloat16
NEG_BIG = -1e30


def _dot(a, b):
    return jnp.dot(a, b, preferred_element_type=F32)


def _dot_nt(a, b):
    return lax.dot_general(a, b, (((1,), (1,)), ((), ())), preferred_element_type=F32)


def _dot_tn(a, b):
    return lax.dot_general(a, b, (((0,), (0,)), ((), ())), preferred_element_type=F32)


def _split3(x):
    hi = x.astype(BF16)
    r = x - hi.astype(F32)
    mid = r.astype(BF16)
    lo = (r - mid.astype(F32)).astype(BF16)
    return hi, mid, lo


def _rms(x, g):
    return x * lax.rsqrt(jnp.mean(x * x, axis=-1, keepdims=True) + EPS) * g


def _group_rms(x, ind, width):
    sq = x * x
    hi = sq.astype(BF16)
    lo = (sq - hi.astype(F32)).astype(BF16)
    ms = (_dot(hi, ind) + _dot(lo, ind)) * (1.0 / width)
    return x * lax.rsqrt(ms + EPS)


def _sigmoid(x):
    return 1.0 / (1.0 + jnp.exp(-x))


def _rope_kernel(pos_ref, inv_ref, tab_ref):
    ang = pos_ref[...].astype(F32) * inv_ref[...]
    lane = lax.broadcasted_iota(jnp.int32, ang.shape, 1)
    c = jnp.cos(ang)
    s = jnp.sin(ang)
    tab_ref[...] = jnp.where(lane < 64, c, jnp.where(lane < 96, -s, s))


def _rope_table(positions, t):
    tr = min(t, 1024)
    inv = 1.0 / (ROPE_THETA ** (jnp.arange(0, MLA_ROPE, 2, dtype=F32) / MLA_ROPE))
    inv4 = jnp.tile(inv, 4).reshape(1, 128)
    return pl.pallas_call(
        _rope_kernel,
        grid=(t // tr,),
        in_specs=[pl.BlockSpec((tr, 1), lambda i: (i, 0)),
                  pl.BlockSpec((1, 128), lambda i: (0, 0))],
        out_specs=pl.BlockSpec((tr, 128), lambda i: (i, 0)),
        out_shape=jax.ShapeDtypeStruct((t, 128), F32),
        name="rope_table",
    )(positions.reshape(t, 1), inv4)


def _mixer_kernel(x_ref, tab_ref, ang_ref, win_ref, wtail_ref, convw_ref, convg_ref, wg_ref, bg_ref, glag_ref,
                  qg_ref, wuq_ref, kvg_ref, wukv_ref, ind_ref, tribl_ref, expand_ref, hmask_ref,
                  ycg_ref, q_ref, kt_ref, v_ref,
                  vbuf, st_ref, ut_ref, sb_ref, *, tm):
    nb = tm // GLA_BLOCK

    @pl.when(pl.program_id(0) == 0)
    def _():
        vbuf[0:8, :] = jnp.zeros((8, CONV_CH), F32)
        st_ref[...] = jnp.zeros(st_ref.shape, F32)

    h = _rms(x_ref[...], ang_ref[...]).astype(BF16)

    def proj(c0, n):
        return _dot(h, win_ref[:, c0:c0 + n])

    def proj_tail(c0, n):
        return _dot(h, wtail_ref[:, c0:c0 + n])

    ind64 = ind_ref[...]

    zc = proj(C_CONV, 3 * CONV_CH)
    cb = zc[:, 0:CONV_CH]
    v = zc[:, CONV_CH:2 * CONV_CH] * zc[:, 2 * CONV_CH:3 * CONV_CH]
    vbuf[8:8 + tm, :] = v
    y = (convw_ref[2:3, :] * v + convw_ref[1:2, :] * vbuf[7:7 + tm, :]
         + convw_ref[0:1, :] * vbuf[6:6 + tm, :])
    vbuf[0:8, :] = vbuf[tm:tm + 8, :]
    y_conv = _group_rms(cb * y, ind64, CONV_GROUP_W) * convg_ref[...]
    ycg_ref[:, 0:CONV_CH] = y_conv.astype(BF16)

    zqk = proj(C_QK, 2 * GLA_QK)
    gq = zqk[:, 0:GLA_QK] * (GLA_DK ** -0.5)
    gk = zqk[:, GLA_QK:2 * GLA_QK]
    gv = proj(C_GV, GLA_VW)
    glow = proj_tail(T_GLOW, 128)
    xg = _dot(glow.astype(BF16), wg_ref[...]) + bg_ref[...]
    la = (jnp.minimum(xg, 0.0) - jnp.log(1.0 + jnp.exp(-jnp.abs(xg)))) * (1.0 / GLA_TAU)

    cat = jnp.concatenate(_split3(la), axis=1)
    tri_blk = tribl_ref[...]
    b_rows, e_rows = [], []
    for g in range(tm // 128):
        r3 = _dot(tri_blk, cat[g * 128:(g + 1) * 128])
        r = r3[:, 0:128] + r3[:, 128:256] + r3[:, 256:384]
        b_rows.append(r[0:128])
        e_rows.append(r[128:256])
    b = jnp.concatenate(b_rows, axis=0)
    bend = jnp.concatenate(e_rows, axis=0)

    expand = expand_ref[...]
    tpos = lax.broadcasted_iota(jnp.int32, (tm, GLA_QK), 0) % GLA_BLOCK
    od = _dot((gq * gk).astype(BF16), expand) * gv
    for d in range(1, GLA_BLOCK):
        arg = jnp.where(tpos >= d, b - pltpu.roll(b, d, axis=0), NEG_BIG)
        p = gq * jnp.exp(arg) * pltpu.roll(gk, d, axis=0)
        od = od + _dot(p.astype(BF16), expand) * pltpu.roll(gv, d, axis=0)

    qs = (gq * jnp.exp(b)).astype(BF16)
    ks = (gk * jnp.exp(bend - b)).astype(BF16)
    vs = gv.astype(BF16)
    dec = jnp.exp(bend)
    head_mask = hmask_ref[...]
    for j in range(nb):
        rs = slice(j * GLA_BLOCK, (j + 1) * GLA_BLOCK)
        ut_ref[j] = _dot_tn(vs[rs], ks[rs]) * head_mask
    st = st_ref[...]
    for j in range(nb):
        sb_ref[j] = st.astype(BF16)
        st = st * dec[j * GLA_BLOCK:j * GLA_BLOCK + 1, :] + ut_ref[j]
    st_ref[...] = st
    o_rows = [_dot_nt(qs[j * GLA_BLOCK:(j + 1) * GLA_BLOCK], sb_ref[j]) for j in range(nb)]

    gr = proj_tail(T_GR, GLA_VW)
    o = jnp.concatenate(o_rows, axis=0) + od
    y_gla = _group_rms(o, ind64, GLA_DV) * glag_ref[...] * (gr * _sigmoid(gr))
    ycg_ref[:, CONV_CH:CONV_CH + GLA_VW] = y_gla.astype(BF16)

    tab = tab_ref[...]
    qtab = tab * ATTN_LOG2_SCALE
    qa = _dot(_rms(proj_tail(T_CQ, MLA_Q_RANK), qg_ref[...]).astype(BF16), wuq_ref[...])
    kva = _dot(_rms(proj_tail(T_CKV, MLA_KV_RANK), kvg_ref[...]).astype(BF16), wukv_ref[...])
    kr = proj_tail(T_KR, 128) * tab
    lane = lax.broadcasted_iota(jnp.int32, (tm, 128), 1)
    kro = jnp.where(lane < MLA_ROPE, kr + pltpu.roll(kr, 64, axis=1), 0.0)
    kro_t = kro.T.astype(BF16)
    nkt = tm // KT_BLOCK
    for hd in range(MLA_HEADS):
        c0 = hd * MLA_QK_PAD
        q_ref[:, c0:c0 + 128] = (qa[:, c0:c0 + 128] * ATTN_LOG2_SCALE).astype(BF16)
        qr = qa[:, c0 + 128:c0 + 256] * qtab
        q_ref[:, c0 + 128:c0 + 256] = (qr + pltpu.roll(qr, 64, axis=1)).astype(BF16)
        knt = kva[:, c0:c0 + 128].T.astype(BF16)
        for part in range(nkt):
            cs = slice(part * KT_BLOCK, (part + 1) * KT_BLOCK)
            kt_ref[hd, part, 0:128, :] = knt[:, cs]
            kt_ref[hd, part, 128:256, :] = kro_t[:, cs]
        v_ref[:, hd * MLA_V:(hd + 1) * MLA_V] = kva[:, c0 + 128:c0 + 256].astype(BF16)


def _mixer_constants():
    idx = lambda n: jnp.arange(n, dtype=jnp.int32)
    ind64 = (idx(256)[:, None] // 64 == idx(256)[None, :] // 64).astype(BF16)
    same = idx(128)[:, None] // GLA_BLOCK == idx(128)[None, :] // GLA_BLOCK
    tri = same & (idx(128)[None, :] <= idx(128)[:, None])
    tri_blk = jnp.concatenate([tri, same], axis=0).astype(BF16)
    expand = (idx(GLA_QK)[:, None] // GLA_DK == idx(GLA_VW)[None, :] // GLA_DV).astype(BF16)
    head_mask = (idx(GLA_VW)[:, None] // GLA_DV == idx(GLA_QK)[None, :] // GLA_DK).astype(F32)
    return ind64, tri_blk, expand, head_mask


def _mixer_in(x, tab, pw, layer, t, tm):
    row = lambda w: pl.BlockSpec((tm, w), lambda i: (i, 0))
    full = lambda a: pl.BlockSpec(a.shape, lambda i: (0,) * a.ndim)
    of_layer = lambda a, cols=None: pl.BlockSpec((None, a.shape[1], cols or a.shape[2]),
                                                 lambda i: (layer, 0, 0))
    stacked = [pw["attn_norm_g"], pw["w_in"], pw["w_in_tail"], pw["conv_w"], pw["conv_norm_g"],
               pw["gla_w_gate"], pw["gla_b_gate"], pw["gla_norm_g"], pw["mla_q_norm_g"], pw["mla_w_uq"],
               pw["mla_kv_norm_g"], pw["mla_w_ukv"]]
    stacked_specs = [of_layer(a, D_IN_HEAD if a is pw["w_in"] else None) for a in stacked]
    consts = list(_mixer_constants())
    return pl.pallas_call(
        functools.partial(_mixer_kernel, tm=tm),
        grid=(t // tm,),
        in_specs=[row(D_MODEL), row(128)] + stacked_specs + [full(a) for a in consts],
        out_specs=[row(CONV_CH + GLA_VW), row(MLA_HEADS * MLA_QK_PAD),
                   pl.BlockSpec((MLA_HEADS, tm // KT_BLOCK, MLA_QK_PAD, KT_BLOCK), lambda i: (0, i, 0, 0)),
                   row(MLA_HEADS * MLA_V)],
        out_shape=[jax.ShapeDtypeStruct((t, CONV_CH + GLA_VW), BF16),
                   jax.ShapeDtypeStruct((t, MLA_HEADS * MLA_QK_PAD), BF16),
                   jax.ShapeDtypeStruct((MLA_HEADS, t // KT_BLOCK, MLA_QK_PAD, KT_BLOCK), BF16),
                   jax.ShapeDtypeStruct((t, MLA_HEADS * MLA_V), BF16)],
        scratch_shapes=[pltpu.VMEM((tm + 8, CONV_CH), F32),
                        pltpu.VMEM((GLA_VW, GLA_QK), F32),
                        pltpu.VMEM((tm // GLA_BLOCK, GLA_VW, GLA_QK), F32),
                        pltpu.VMEM((tm // GLA_BLOCK, GLA_VW, GLA_QK), BF16)],
        compiler_params=pltpu.CompilerParams(dimension_semantics=("arbitrary",),
                                             vmem_limit_bytes=V7X_VMEM_LIMIT_BYTES),
        name="mixer_in",
    )(x, tab, *stacked, *consts)


def _attn_kernel(q_ref, kt_ref, v_ref, g_ref, o_ref, s0, s1, p0, p1, al0, al1, m_sc, l_sc, acc_sc,
                 *, bq, bk):
    i = pl.program_id(1)
    nch = bk // 128
    s_bufs, p_bufs, al_bufs = (s0, s1), (p0, p1), (al0, al1)
    m_sc[...] = jnp.full(m_sc.shape, -jnp.inf, F32)
    l_sc[...] = jnp.zeros(l_sc.shape, F32)
    acc_sc[...] = jnp.zeros(acc_sc.shape, F32)

    def scores(j, slot, row0):
        s_bufs[slot][row0:, 0:bk] = _dot(q_ref[row0:, :], kt_ref[j])

    def softmax(slot, diag):
        s_ref, p_ref, al_ref = s_bufs[slot], p_bufs[slot], al_bufs[slot]
        col0 = 0 if diag is None else diag * bk
        for r in range(col0 // ATTN_ROWS, bq // ATTN_ROWS):
            lo, hi = r * ATTN_ROWS, (r + 1) * ATTN_ROWS
            rs = slice(lo, hi)
            live = nch if diag is None else min(nch, (hi - 1 - col0) // 128 + 1)
            need_mask = [diag is not None and col0 + (c + 1) * 128 - 1 > lo for c in range(live)]
            if any(need_mask):
                rows = lax.broadcasted_iota(jnp.int32, (ATTN_ROWS, 128), 0) + (lo - col0)
                cols = lax.broadcasted_iota(jnp.int32, (ATTN_ROWS, 128), 1)

            def load(c):
                ch = s_ref[rs, c * 128:(c + 1) * 128]
                if need_mask[c]:
                    ch = jnp.where(cols + c * 128 <= rows, ch, -jnp.inf)
                return ch

            mx = functools.reduce(jnp.maximum, [load(c) for c in range(live)])
            m_prev = m_sc[rs, :]
            m_new = jnp.maximum(m_prev, jnp.max(mx, axis=-1, keepdims=True))
            alpha = jnp.exp2(m_prev - m_new)
            m_sc[rs, :] = m_new
            al_ref[rs, :] = alpha
            psum = None
            for c in range(nch):
                if c < live:
                    p = jnp.exp2(load(c) - m_new)
                    p_ref[rs, c * 128:(c + 1) * 128] = p.astype(BF16)
                    psum = p if psum is None else psum + p
                else:
                    p_ref[rs, c * 128:(c + 1) * 128] = jnp.zeros((ATTN_ROWS, 128), BF16)
            l_sc[rs, :] = alpha * l_sc[rs, :] + psum

    def pv(j, slot, row0):
        r0 = pl.multiple_of(j * bk, bk)
        acc_sc[row0:, :] = (al_bufs[slot][row0:, :] * acc_sc[row0:, :]
                            + _dot(p_bufs[slot][row0:, 0:bk], v_ref[pl.ds(r0, bk), :]))

    def pair(jj, carry):
        for a in range(2):
            scores(2 * jj + a, a, 0)
        for a in range(2):
            softmax(a, None)
            pv(2 * jj + a, a, 0)
        return carry

    lax.fori_loop(0, i, pair, 0)

    for c in range(2):
        scores(2 * i + c, c, c * bk)
    for c in range(2):
        softmax(c, c)
        pv(2 * i + c, c, c * bk)

    o = acc_sc[...] / jnp.sum(l_sc[...], axis=-1, keepdims=True)
    o_ref[...] = _rms(o, g_ref[...]).astype(BF16)


def _mla_attn(q, kt, v, out_norm_g, layer, t, bq, bk):
    assert bq == 2 * bk
    whole = functools.partial(pl.BlockSpec, pipeline_mode=pl.Buffered(1))
    return pl.pallas_call(
        functools.partial(_attn_kernel, bq=bq, bk=bk),
        grid=(MLA_HEADS, t // bq),
        in_specs=[pl.BlockSpec((bq, MLA_QK_PAD), lambda h, i: (i, h)),
                  whole((None, t // bk, MLA_QK_PAD, bk), lambda h, i: (h, 0, 0, 0)),
                  whole((t, MLA_V), lambda h, i: (0, h)),
                  pl.BlockSpec((None, 1, MLA_V), lambda h, i: (layer, 0, h))],
        out_specs=pl.BlockSpec((bq, MLA_V), lambda h, i: (i, h)),
        out_shape=jax.ShapeDtypeStruct((t, MLA_HEADS * MLA_V), BF16),
        scratch_shapes=[pltpu.VMEM((bq, bk + 128), F32), pltpu.VMEM((bq, bk + 128), F32),
                        pltpu.VMEM((bq, bk + 128), BF16), pltpu.VMEM((bq, bk + 128), BF16),
                        pltpu.VMEM((bq, 128), F32), pltpu.VMEM((bq, 128), F32),
                        pltpu.VMEM((bq, 128), F32), pltpu.VMEM((bq, 128), F32),
                        pltpu.VMEM((bq, MLA_V), F32)],
        compiler_params=pltpu.CompilerParams(dimension_semantics=("arbitrary", "arbitrary"),
                                             vmem_limit_bytes=V7X_VMEM_LIMIT_BYTES),
        name="mla_attn",
    )(q, kt, v, out_norm_g)


def _out_ffn_kernel(x_ref, ycg_ref, ymla_ref, wout_ref, fg_ref, wgate_ref, wup_ref, wdown_ref,
                    og_ref, o_ref, *, final_norm):
    half = CONV_CH + GLA_VW
    x1 = (x_ref[...] + _dot(ycg_ref[...], wout_ref[0:half, :])
          + _dot(ymla_ref[...], wout_ref[half:2 * half, :]))
    h = _rms(x1, fg_ref[...]).astype(BF16)
    down = None
    for c in range(D_FF // FF_CHUNK):
        cs = slice(c * FF_CHUNK, (c + 1) * FF_CHUNK)
        g = _dot(h, wgate_ref[:, cs])
        u = _dot(h, wup_ref[:, cs])
        a = (g * _sigmoid(g) * u).astype(BF16)
        part = _dot(a, wdown_ref[cs, :])
        down = part if down is None else down + part
    out = x1 + down
    if final_norm:
        out = _rms(out, og_ref[...])
    o_ref[...] = out


def _out_ffn(x, ycg, ymla, pw, layer, final_g, t, tm, final_norm):
    row = lambda w: pl.BlockSpec((tm, w), lambda i: (i, 0))
    resident = lambda a: pl.BlockSpec((None,) + a.shape[1:], lambda i: (layer, 0, 0),
                                      pipeline_mode=pl.Buffered(1))
    weights = [pw["w_out"], pw["ffn_norm_g"], pw["w_gate"], pw["w_up"], pw["w_down"]]
    return pl.pallas_call(
        functools.partial(_out_ffn_kernel, final_norm=final_norm),
        grid=(t // tm,),
        in_specs=[row(D_MODEL), row(CONV_CH + GLA_VW), row(MLA_HEADS * MLA_V)]
                 + [resident(a) for a in weights] + [pl.BlockSpec(final_g.shape, lambda i: (0, 0))],
        out_specs=row(D_MODEL),
        out_shape=jax.ShapeDtypeStruct((t, D_MODEL), F32),
        compiler_params=pltpu.CompilerParams(dimension_semantics=("arbitrary",),
                                             vmem_limit_bytes=V7X_VMEM_LIMIT_BYTES),
        name="out_ffn",
    )(x, ycg, ymla, *weights, final_g)


def _pack_w_in_tail(w):
    glow = jnp.pad(w[..., 1280:1296], ((0, 0), (0, 0), (0, 128 - GLA_GATE_RANK)))
    return jnp.concatenate(
        [w[..., 1296:1552], glow, w[..., 1552:1808], w[..., 1808:1936], w[..., 1936:2000],
         w[..., 1968:2000], w[..., 1936:1968]], axis=-1)


def _pack_w_uq(w):
    depth = w.shape[0]
    w = w.astype(BF16).reshape(depth, MLA_Q_RANK, MLA_HEADS, MLA_NOPE + MLA_ROPE)
    nope, r1, r2 = w[..., :MLA_NOPE], w[..., MLA_NOPE:MLA_NOPE + 32], w[..., MLA_NOPE + 32:]
    return jnp.concatenate([nope, r1, r2, r2, r1], axis=-1).reshape(
        depth, MLA_Q_RANK, MLA_HEADS * MLA_QK_PAD)


def kernel(x, positions, attn_norm_g, w_in, conv_w, conv_norm_g, gla_w_gate, gla_b_gate, gla_norm_g, mla_q_norm_g, mla_w_uq, mla_kv_norm_g, mla_w_ukv, mla_out_norm_g, w_out, ffn_norm_g, w_gate, w_up, w_down, final_norm_g):
    batch, t, _ = x.shape
    assert batch == 1
    depth = w_in.shape[0]
    tm = min(t, 1024)
    tm_ffn = min(t, 1024)
    bq = min(t, 1024)
    bk = bq // 2
    assert t % tm == 0 and t % bq == 0 and tm % 128 == 0 and bk == KT_BLOCK

    row = lambda a: a.reshape(depth, 1, a.shape[-1])
    w_in_bf16 = w_in.astype(BF16)
    packed = {
        "attn_norm_g": row(attn_norm_g),
        "w_in": w_in_bf16,
        "w_in_tail": _pack_w_in_tail(w_in_bf16),
        "conv_w": jnp.pad(conv_w, ((0, 0), (0, 8 - conv_w.shape[1]), (0, 0))),
        "conv_norm_g": row(conv_norm_g),
        "gla_w_gate": jnp.pad(gla_w_gate, ((0, 0), (0, 128 - GLA_GATE_RANK), (0, 0))).astype(BF16),
        "gla_b_gate": row(gla_b_gate),
        "gla_norm_g": row(gla_norm_g),
        "mla_q_norm_g": row(mla_q_norm_g),
        "mla_w_uq": _pack_w_uq(mla_w_uq),
        "mla_kv_norm_g": row(mla_kv_norm_g),
        "mla_w_ukv": mla_w_ukv.astype(BF16),
        "mla_out_norm_g": row(mla_out_norm_g),
        "w_out": w_out.astype(BF16),
        "ffn_norm_g": row(ffn_norm_g),
        "w_gate": w_gate.astype(BF16),
        "w_up": w_up.astype(BF16),
        "w_down": w_down.astype(BF16),
    }
    final_g = final_norm_g.reshape(1, D_MODEL)

    tab = _rope_table(positions, t)
    xs = x.reshape(t, D_MODEL)
    for l in range(depth):
        ycg, q, kt, v = _mixer_in(xs, tab, packed, l, t, tm)
        ymla = _mla_attn(q, kt, v, packed["mla_out_norm_g"], l, t, bq, bk)
        xs = _out_ffn(xs, ycg, ymla, packed, l, final_g, t, tm_ffn, final_norm=(l == depth - 1))
    return xs.reshape(batch, t, D_MODEL)
```
